```python
import jax, jax.numpy as jnp
from jax import lax
import numpy as np

D_MODEL = 2048
BATCH = 2
SEQ = 16384
DEPTH = 1
DEC_BATCH = 32
DEC_SEQ = 32
PAST_LEN = 4096

CHUNK = 64
Q_BLOCK = 128
NORM_EPS = 1e-6
MLA_NOPE = 128
MLA_ROPE = 64
MLA_V = 128
MLA_HEADS = D_MODEL // MLA_V
MLA_Q_LORA = 512
MLA_KV_LORA = 512
MLA_SCALE = (MLA_NOPE + MLA_ROPE) ** -0.5
ROPE_THETA = 10000.0
FOX_HEAD_DIM = 128
FOX_HEADS = D_MODEL // FOX_HEAD_DIM
FOX_SCALE = FOX_HEAD_DIM ** -0.5
C_IN = MLA_Q_LORA + MLA_KV_LORA + MLA_ROPE + 3 * FOX_HEADS * FOX_HEAD_DIM + FOX_HEADS + 2 * D_MODEL
N_EXPERTS = 32
TOP_K = 4
D_FF = D_MODEL
SWIGLU_LIMIT = 7.0
SWIGLU_ALPHA = 1.702
EXPERT_BLOCK = 128
PLE_DIM = 256

kernel_name = 'hybrid_mla_fox_moe_stream_step'


def rmsnorm(x, g):
    x32 = x.astype(jnp.float32)
    y = x32 * lax.rsqrt(jnp.mean(x32 * x32, axis=-1, keepdims=True) + NORM_EPS)
    return (y * g.astype(jnp.float32)).astype(x.dtype)


def rope(x, pos):
    half = x.shape[-1] // 2
    freqs = ROPE_THETA ** (-jnp.arange(half, dtype=jnp.float32) / half)
    ang = pos.astype(jnp.float32)[:, None] * freqs[None, :]
    cos = jnp.cos(ang)[None, :, None, :]
    sin = jnp.sin(ang)[None, :, None, :]
    x32 = x.astype(jnp.float32)
    x1, x2 = x32[..., :half], x32[..., half:]
    return jnp.concatenate([x1 * cos - x2 * sin, x1 * sin + x2 * cos], axis=-1).astype(x.dtype)


def mixer_inputs(h, pos, w_in, b_forget, b_gate, g_q_lat, w_q_up, g_kv_lat):
    B, T, _ = h.shape
    z = h @ w_in
    sizes = [MLA_Q_LORA, MLA_KV_LORA, MLA_ROPE, FOX_HEADS * FOX_HEAD_DIM, FOX_HEADS * FOX_HEAD_DIM,
             FOX_HEADS * FOX_HEAD_DIM, FOX_HEADS, 2 * D_MODEL]
    parts = []
    off = 0
    for s in sizes:
        parts.append(z[..., off:off + s])
        off += s
    cq, ckv, kr, fq, fk, fv, fgate, gate = parts
    q = jnp.einsum('btc,chd->bthd', rmsnorm(cq, g_q_lat), w_q_up)
    q_nope = q[..., :MLA_NOPE]
    q_rope = rope(q[..., MLA_NOPE:], pos)
    c_lat = rmsnorm(ckv, g_kv_lat)
    k_rope = rope(kr[:, :, None, :], pos)[:, :, 0, :]
    fq = fq.reshape(B, T, FOX_HEADS, FOX_HEAD_DIM)
    fk = fk.reshape(B, T, FOX_HEADS, FOX_HEAD_DIM)
    fv = fv.reshape(B, T, FOX_HEADS, FOX_HEAD_DIM)
    logf = jax.nn.log_sigmoid(fgate.astype(jnp.float32) + b_forget.astype(jnp.float32))
    gates = jax.nn.sigmoid(gate + b_gate)
    return q_nope, q_rope, c_lat, k_rope, fq, fk, fv, logf, gates[..., :D_MODEL], gates[..., D_MODEL:]


def blocked_attention(q, k, v, scale, logit_fn):
    B, T, H, dk = q.shape
    dv = v.shape[-1]
    nb = T // Q_BLOCK
    q_blocks = q.reshape(B, nb, Q_BLOCK, H, dk).swapaxes(0, 1)

    def one_block(args):
        q_blk, blk = args
        q_pos = blk * Q_BLOCK + jnp.arange(Q_BLOCK)
        s = jnp.einsum('bqhd,bkhd->bhqk', q_blk, k, preferred_element_type=jnp.float32) * scale
        p = jax.nn.softmax(logit_fn(s, q_pos, blk), axis=-1).astype(v.dtype)
        return jnp.einsum('bhqk,bkhd->bqhd', p, v)

    out = lax.map(one_block, (q_blocks, jnp.arange(nb)))
    return out.swapaxes(0, 1).reshape(B, T, H * dv)


def mla_prompt(q_nope, q_rope, c_lat, k_rope, w_uk, w_uv):
    B, T, H, _ = q_nope.shape
    k_nope = jnp.einsum('btc,chd->bthd', c_lat, w_uk)
    v = jnp.einsum('btc,chd->bthd', c_lat, w_uv)
    q = jnp.concatenate([q_nope, q_rope], axis=-1)
    k = jnp.concatenate([k_nope, jnp.broadcast_to(k_rope[:, :, None, :], (B, T, H, MLA_ROPE))], axis=-1)
    k_chunk = jnp.arange(T) // CHUNK

    def logits(s, q_pos, blk):
        mask = k_chunk[None, :] <= (q_pos // CHUNK)[:, None]
        return jnp.where(mask, s, -jnp.inf)

    return blocked_attention(q, k, v, MLA_SCALE, logits)


def mla_sample(q_nope, q_rope, c_lat, k_rope, cache_lat, cache_rope, w_uk, w_uv):
    B, T, H, _ = q_nope.shape
    c_all = jnp.concatenate([cache_lat.astype(c_lat.dtype), c_lat], axis=1)
    r_all = jnp.concatenate([cache_rope.astype(k_rope.dtype), k_rope], axis=1)
    q_lat = jnp.einsum('bthd,chd->bthc', q_nope, w_uk)
    s = (jnp.einsum('bthc,bsc->bhts', q_lat, c_all, preferred_element_type=jnp.float32)
         + jnp.einsum('bthr,bsr->bhts', q_rope, r_all, preferred_element_type=jnp.float32)) * MLA_SCALE
    p = jax.nn.softmax(s, axis=-1).astype(c_all.dtype)
    o_lat = jnp.einsum('bhts,bsc->bthc', p, c_all)
    return jnp.einsum('bthc,chd->bthd', o_lat, w_uv).reshape(B, T, H * MLA_V)


def fox_prompt(fq, fk, fv, logf):
    T = fq.shape[1]
    cum = jnp.cumsum(logf, axis=1).transpose(0, 2, 1)
    k_pos = jnp.arange(T)

    def logits(s, q_pos, blk):
        cum_q = lax.dynamic_slice_in_dim(cum, blk * Q_BLOCK, Q_BLOCK, axis=2)
        bias = cum_q[..., :, None] - cum[..., None, :]
        mask = k_pos[None, :] <= q_pos[:, None]
        return jnp.where(mask, s + bias, -jnp.inf)

    return blocked_attention(fq, fk, fv, FOX_SCALE, logits)


def fox_sample(fq, fk, fv, logf, cache_k, cache_v, cache_logf):
    B, T, H, _ = fq.shape
    P = cache_k.shape[1]
    k_all = jnp.concatenate([cache_k.astype(fk.dtype), fk], axis=1)
    v_all = jnp.concatenate([cache_v.astype(fv.dtype), fv], axis=1)
    cum = jnp.cumsum(jnp.concatenate([cache_logf.astype(jnp.float32), logf], axis=1), axis=1).transpose(0, 2, 1)
    s = jnp.einsum('bqhd,bkhd->bhqk', fq, k_all, preferred_element_type=jnp.float32) * FOX_SCALE
    s = s + cum[..., P:, None] - cum[..., None, :]
    mask = jnp.arange(P + T)[None, :] <= (P + jnp.arange(T))[:, None]
    p = jax.nn.softmax(jnp.where(mask, s, -jnp.inf), axis=-1).astype(v_all.dtype)
    return jnp.einsum('bhqk,bkhd->bqhd', p, v_all).reshape(B, T, H * FOX_HEAD_DIM)


def moe(h, w_router, b_router, w_gu, b_gu, w_down, b_down):
    shape = h.shape
    hf = h.reshape(-1, shape[-1])
    N = hf.shape[0]
    logits = (hf @ w_router).astype(jnp.float32) + b_router.astype(jnp.float32)
    top_v, top_i = lax.top_k(logits, TOP_K)
    gates = jax.nn.softmax(top_v, axis=-1)
    NK = N * TOP_K
    a_exp = top_i.reshape(-1)
    a_tok = jnp.repeat(jnp.arange(N, dtype=jnp.int32), TOP_K)
    a_w = gates.reshape(-1)
    order = jnp.argsort(a_exp)
    s_exp = a_exp[order]
    counts = jnp.bincount(a_exp, length=N_EXPERTS)
    starts = jnp.cumsum(counts) - counts
    padded = (counts + EXPERT_BLOCK - 1) // EXPERT_BLOCK * EXPERT_BLOCK
    pends = jnp.cumsum(padded)
    pstarts = pends - padded
    dest = pstarts[s_exp] + (jnp.arange(NK) - starts[s_exp])
    n_blocks = -(-(NK + N_EXPERTS * (EXPERT_BLOCK - 1)) // EXPERT_BLOCK)
    P = n_blocks * EXPERT_BLOCK
    buf_tok = jnp.full((P,), N, jnp.int32).at[dest].set(a_tok[order])
    buf_w = jnp.zeros((P,), jnp.float32).at[dest].set(a_w[order])
    block_exp = jnp.clip(jnp.searchsorted(pends, jnp.arange(n_blocks) * EXPERT_BLOCK, side='right'),
                         0, N_EXPERTS - 1)
    hpad = jnp.concatenate([hf, jnp.zeros((1, hf.shape[1]), hf.dtype)], axis=0)

    def run_block(args):
        tok, e = args
        xb = hpad[tok]
        gu = xb @ w_gu[e] + b_gu[e]
        g = jnp.minimum(gu[:, :D_FF], SWIGLU_LIMIT)
        u = jnp.clip(gu[:, D_FF:], -SWIGLU_LIMIT, SWIGLU_LIMIT)
        act = g * jax.nn.sigmoid(SWIGLU_ALPHA * g) * (u + 1)
        return act @ w_down[e] + b_down[e]

    out = lax.map(run_block, (buf_tok.reshape(n_blocks, EXPERT_BLOCK), block_exp))
    out = out.reshape(P, -1) * buf_w[:, None].astype(out.dtype)
    y = jnp.zeros((N + 1, out.shape[-1]), out.dtype).at[buf_tok].add(out)[:N]
    return y.reshape(shape).astype(h.dtype)


def trunk_layer(x, p_emb, pos, caches, g_mix, w_in, b_forget, b_gate, g_q_lat, w_q_up, g_kv_lat,
                w_uk, w_uv, w_o, g_ffn, w_router, b_router, w_gu, b_gu, w_down, b_down, w_pl, w_plg, b_plg):
    h = rmsnorm(x, g_mix)
    q_nope, q_rope, c_lat, k_rope, fq, fk, fv, logf, g_a, g_b = mixer_inputs(
        h, pos, w_in, b_forget, b_gate, g_q_lat, w_q_up, g_kv_lat)
    if caches is None:
        o_a = mla_prompt(q_nope, q_rope, c_lat, k_rope, w_uk, w_uv)
        o_b = fox_prompt(fq, fk, fv, logf)
    else:
        cache_lat, cache_rope, cache_k, cache_v, cache_logf = caches
        o_a = mla_sample(q_nope, q_rope, c_lat, k_rope, cache_lat, cache_rope, w_uk, w_uv)
        o_b = fox_sample(fq, fk, fv, logf, cache_k, cache_v, cache_logf)
    x = x + (g_a * o_a + g_b * o_b) @ w_o
    x = x + moe(rmsnorm(x, g_ffn), w_router, b_router, w_gu, b_gu, w_down, b_down)
    x = x + jax.nn.sigmoid(x @ w_plg + b_plg) * (p_emb @ w_pl)
    return x, (c_lat, k_rope, fk, fv, logf)


def setup_inputs(seed: int = 0) -> dict:
    key = jax.random.key(seed)
    ks = iter(jax.random.split(key, 48))

    def nrm(shape, scale):
        return jax.random.normal(next(ks), shape, jnp.float32) * scale

    L, D = DEPTH, D_MODEL
    return {
        'x_prompt': nrm((BATCH, SEQ, D), 1.0),
        'x_sample': nrm((DEC_BATCH, DEC_SEQ, D), 1.0),
        'p_prompt': nrm((L, BATCH, SEQ, PLE_DIM), 1.0),
        'p_sample': nrm((L, DEC_BATCH, DEC_SEQ, PLE_DIM), 1.0),
        'cache_mla_latent': nrm((L, DEC_BATCH, PAST_LEN, MLA_KV_LORA), 1.0),
        'cache_mla_rope': nrm((L, DEC_BATCH, PAST_LEN, MLA_ROPE), 1.0),
        'cache_fox_k': nrm((L, DEC_BATCH, PAST_LEN, FOX_HEADS, FOX_HEAD_DIM), 1.0),
        'cache_fox_v': nrm((L, DEC_BATCH, PAST_LEN, FOX_HEADS, FOX_HEAD_DIM), 1.0),
        'cache_fox_logf': jax.nn.log_sigmoid(4.0 + nrm((L, DEC_BATCH, PAST_LEN, FOX_HEADS), 0.5)),
        'g_mix': 1.0 + nrm((L, D), 0.02),
        'w_in': nrm((L, D, C_IN), D ** -0.5),
        'b_forget': 4.0 + nrm((L, FOX_HEADS), 0.1),
        'b_gate': nrm((L, 2 * D), 0.02),
        'g_q_lat': 1.0 + nrm((L, MLA_Q_LORA), 0.02),
        'w_q_up': nrm((L, MLA_Q_LORA, MLA_HEADS, MLA_NOPE + MLA_ROPE), MLA_Q_LORA ** -0.5),
        'g_kv_lat': 1.0 + nrm((L, MLA_KV_LORA), 0.02),
        'w_uk': nrm((L, MLA_KV_LORA, MLA_HEADS, MLA_NOPE), MLA_KV_LORA ** -0.5),
        'w_uv': nrm((L, MLA_KV_LORA, MLA_HEADS, MLA_V), MLA_KV_LORA ** -0.5),
        'w_o': nrm((L, D, D), D ** -0.5),
        'g_ffn': 1.0 + nrm((L, D), 0.02),
        'w_router': nrm((L, D, N_EXPERTS), D ** -0.5),
        'b_router': nrm((L, N_EXPERTS), 0.01),
        'w_gu': nrm((L, N_EXPERTS, D, 2 * D_FF), D ** -0.5),
        'b_gu': nrm((L, N_EXPERTS, 2 * D_FF), 0.02),
        'w_down': nrm((L, N_EXPERTS, D_FF, D), D_FF ** -0.5),
        'b_down': nrm((L, N_EXPERTS, D), 0.02),
        'w_pl': nrm((L, PLE_DIM, D), PLE_DIM ** -0.5),
        'w_plg': nrm((L, D, D), D ** -0.5),
        'b_plg': nrm((L, D), 0.02),
        'g_final': 1.0 + nrm((D,), 0.02),
    }


def reference(x_prompt, x_sample, p_prompt, p_sample, cache_mla_latent, cache_mla_rope, cache_fox_k,
              cache_fox_v, cache_fox_logf, g_mix, w_in, b_forget, b_gate, g_q_lat, w_q_up, g_kv_lat,
              w_uk, w_uv, w_o, g_ffn, w_router, b_router, w_gu, b_gu, w_down, b_down, w_pl, w_plg,
              b_plg, g_final):
    pos_p = jnp.arange(x_prompt.shape[1])
    pos_s = cache_mla_latent.shape[2] + jnp.arange(x_sample.shape[1])
    y_p, y_s = x_prompt, x_sample
    st_p, st_s = [], []
    for i in range(DEPTH):
        w = (g_mix[i], w_in[i], b_forget[i], b_gate[i], g_q_lat[i], w_q_up[i], g_kv_lat[i], w_uk[i],
             w_uv[i], w_o[i], g_ffn[i], w_router[i], b_router[i], w_gu[i], b_gu[i], w_down[i], b_down[i],
             w_pl[i], w_plg[i], b_plg[i])
        y_p, sp = trunk_layer(y_p, p_prompt[i], pos_p, None, *w)
        caches = (cache_mla_latent[i], cache_mla_rope[i], cache_fox_k[i], cache_fox_v[i], cache_fox_logf[i])
        y_s, ss = trunk_layer(y_s, p_sample[i], pos_s, caches, *w)
        st_p.append(sp)
        st_s.append(ss)
    y_prompt = rmsnorm(y_p, g_final)
    y_sample = rmsnorm(y_s, g_final)
    return (y_prompt, y_sample,
            jnp.stack([s[0] for s in st_p]), jnp.stack([s[1] for s in st_p]),
            jnp.stack([s[2] for s in st_p]), jnp.stack([s[3] for s in st_p]),
            jnp.stack([s[4] for s in st_p]),
            jnp.stack([s[0] for s in st_s]), jnp.stack([s[1] for s in st_s]),
            jnp.stack([s[2] for s in st_s]), jnp.stack([s[3] for s in st_s]),
            jnp.stack([s[4] for s in st_s]))
```

```python
import functools
import math

import jax
import jax.numpy as jnp
from jax import lax
from jax.experimental import pallas as pl
from jax.experimental.pallas import tpu as pltpu

CHUNK = 64
NORM_EPS = 1e-6
HEAD = 128
ROPE = 64
QK_PAD = 256
MLA_SCALE = (HEAD + ROPE) ** -0.5
FOX_SCALE = HEAD ** -0.5
ROPE_THETA = 10000.0
TOP_K = 4
SWIGLU_LIMIT = 7.0
SWIGLU_ALPHA = 1.702
LANE = 128
VMEM_LIMIT = 56 * 1024 * 1024
NEG = -1e30

BF = jnp.bfloat16
F32 = jnp.float32


def _tile(n, pref):
    if n <= pref:
        return n
    for t in range(pref, 7, -1):
        if n % t == 0 and t % 8 == 0:
            return t
    return n


def _params(sem):
    return pltpu.CompilerParams(dimension_semantics=sem, vmem_limit_bytes=VMEM_LIMIT)


def _rms(x, g):
    return x * lax.rsqrt(jnp.mean(x * x, axis=-1, keepdims=True) + NORM_EPS) * g


def _sigmoid(x):
    return 1.0 / (1.0 + jnp.exp(-x))


def _rmsnorm_kernel(x_ref, g_ref, o_ref):
    o_ref[...] = _rms(x_ref[...], g_ref[...]).astype(o_ref.dtype)


def _rmsnorm(x, g, out_dtype, tm=512):
    m, d = x.shape
    tm = _tile(m, tm)
    return pl.pallas_call(
        _rmsnorm_kernel,
        grid=(m // tm,),
        in_specs=[pl.BlockSpec((tm, d), lambda i: (i, 0)),
                  pl.BlockSpec((1, d), lambda i: (0, 0))],
        out_specs=pl.BlockSpec((tm, d), lambda i: (i, 0)),
        out_shape=jax.ShapeDtypeStruct((m, d), out_dtype),
        compiler_params=_params(("parallel",)),
        name="rmsnorm",
    )(x, g.reshape(1, d))


def _ffn_norm_kernel(x_ref, g_ref, wr_ref, br_ref, xn_ref, lg_ref):
    xn = _rms(x_ref[...], g_ref[...])
    xn_ref[...] = xn.astype(xn_ref.dtype)
    lg_ref[...] = jnp.dot(xn, wr_ref[...], preferred_element_type=F32,
                          precision=lax.Precision.HIGHEST) + br_ref[...]


def _ffn_norm(x, g, w_router, b_router, tm=512):
    m, d = x.shape
    e = w_router.shape[1]
    ep = -(-e // LANE) * LANE
    wr = jnp.pad(w_router, ((0, 0), (0, ep - e)))
    br = jnp.pad(b_router, (0, ep - e)).reshape(1, ep)
    tm = _tile(m, tm)
    xn, lg = pl.pallas_call(
        _ffn_norm_kernel,
        grid=(m // tm,),
        in_specs=[pl.BlockSpec((tm, d), lambda i: (i, 0)),
                  pl.BlockSpec((1, d), lambda i: (0, 0)),
                  pl.BlockSpec((d, ep), lambda i: (0, 0)),
                  pl.BlockSpec((1, ep), lambda i: (0, 0))],
        out_specs=[pl.BlockSpec((tm, d), lambda i: (i, 0)),
                   pl.BlockSpec((tm, ep), lambda i: (i, 0))],
        out_shape=[jax.ShapeDtypeStruct((m, d), BF),
                   jax.ShapeDtypeStruct((m, ep), F32)],
        compiler_params=_params(("parallel",)),
        name="ffn_norm_router",
    )(x, g.reshape(1, d), wr, br)
    return xn, lg[:, :e]


def _mm_kernel(*refs, n_extra, epilogue):
    a_ref, w_ref = refs[0], refs[1]
    extra = refs[2:2 + n_extra]
    outs = refs[2 + n_extra:]
    acc = jnp.dot(a_ref[...], w_ref[...], preferred_element_type=F32)
    epilogue(acc, extra, outs)


def _mm(a, w, epilogue, out_shapes, out_widths, *, tm, tn, extra=(), extra_specs=(),
        a_spec=None, w_spec=None, n_col_tiles=None, name="mm"):
    m = a.shape[0]
    k = a.shape[-1] if a_spec is None else None
    tm = _tile(m, tm)
    if n_col_tiles is None:
        n_col_tiles = w.shape[-1] // tn
    if a_spec is None:
        a_spec = pl.BlockSpec((tm, k), lambda i, j: (i, 0))
    if w_spec is None:
        w_spec = pl.BlockSpec((w.shape[0], tn), lambda i, j: (0, j))
    out_specs = [pl.BlockSpec((tm, ow), lambda i, j: (i, j)) for ow in out_widths]
    return pl.pallas_call(
        functools.partial(_mm_kernel, n_extra=len(extra), epilogue=epilogue),
        grid=(m // tm, n_col_tiles),
        in_specs=[a_spec, w_spec, *extra_specs],
        out_specs=out_specs,
        out_shape=out_shapes,
        compiler_params=_params(("parallel", "arbitrary")),
        name=name,
    )(a, w, *extra)


def _rope_mix(v, cs_ref):
    cs = cs_ref[...]
    lane = lax.broadcasted_iota(jnp.int32, cs.shape, 1)
    c = jnp.where(lane < ROPE, cs, 0.0)
    s = jnp.where(lane < ROPE, pltpu.roll(cs, ROPE, 1), 0.0)
    return v * c + pltpu.roll(v, ROPE, 1) * s


def _latent_epilogue(acc, extra, outs, *, ql, kvl):
    gq_ref, gkv_ref, bf_ref, cs_ref = extra
    cqn_ref, clat_ref, clatb_ref, kr_ref, krb_ref, logf_ref = outs
    cqn_ref[...] = _rms(acc[:, :ql], gq_ref[...]).astype(cqn_ref.dtype)
    clat = _rms(acc[:, ql:ql + kvl], gkv_ref[...])
    clat_ref[...] = clat
    clatb_ref[...] = clat.astype(clatb_ref.dtype)
    o = ql + kvl
    kr = _rope_mix(acc[:, o:o + LANE], cs_ref)
    kr_ref[...] = kr
    krb_ref[...] = kr.astype(krb_ref.dtype)
    fg = acc[:, o + LANE:o + 2 * LANE] + bf_ref[...]
    logf_ref[...] = jnp.minimum(fg, 0.0) - jnp.log(1.0 + jnp.exp(-jnp.abs(fg)))


def _scaled_bf16_epilogue(acc, extra, outs, *, scale):
    outs[0][...] = (acc * scale).astype(outs[0].dtype)


def _dual_epilogue(acc, extra, outs):
    outs[0][...] = acc
    outs[1][...] = acc.astype(outs[1].dtype)


def _gate_epilogue(acc, extra, outs):
    outs[0][...] = _sigmoid(acc + extra[0][...]).astype(outs[0].dtype)


def _q_epilogue(acc, extra, outs, *, heads_per_tile):
    cs_ref = extra[0]
    o_ref = outs[0]
    for h in range(heads_per_tile):
        b = h * QK_PAD
        o_ref[:, b:b + HEAD] = (acc[:, b:b + HEAD] * MLA_SCALE).astype(o_ref.dtype)
        o_ref[:, b + HEAD:b + QK_PAD] = (
            _rope_mix(acc[:, b + HEAD:b + QK_PAD], cs_ref) * MLA_SCALE).astype(o_ref.dtype)


def _k_epilogue(acc, extra, outs, *, heads_per_tile):
    krb_ref = extra[0]
    o_ref = outs[0]
    for h in range(heads_per_tile):
        o_ref[:, h * QK_PAD:h * QK_PAD + HEAD] = acc[:, h * HEAD:(h + 1) * HEAD].astype(o_ref.dtype)
        o_ref[:, h * QK_PAD + HEAD:(h + 1) * QK_PAD] = krb_ref[...]


def _bf16_epilogue(acc, extra, outs):
    outs[0][...] = acc.astype(outs[0].dtype)


def _flash_prompt_kernel(*refs, t, fox):
    if fox:
        q_ref, k_ref, v_ref, cq_ref, ck_ref, o_ref, m_sc, l_sc, acc_sc = refs
    else:
        q_ref, k_ref, v_ref, o_ref, m_sc, l_sc, acc_sc = refs
    i = pl.program_id(2)
    m_sc[...] = jnp.full(m_sc.shape, NEG, F32)
    l_sc[...] = jnp.zeros(l_sc.shape, F32)
    acc_sc[...] = jnp.zeros(acc_sc.shape, F32)
    q = q_ref[...]

    def block(j, masked):
        start = pl.multiple_of(j * t, t)
        k = k_ref[pl.ds(start, t), :]
        v = v_ref[pl.ds(start, t), :]
        s = lax.dot_general(q, k, (((1,), (1,)), ((), ())), preferred_element_type=F32)
        if fox:
            s = s + (cq_ref[...] - ck_ref[:, pl.ds(start, t)])
        if masked:
            qp = lax.broadcasted_iota(jnp.int32, s.shape, 0)
            kp = lax.broadcasted_iota(jnp.int32, s.shape, 1)
            if fox:
                keep = kp <= qp
            else:
                keep = (kp // CHUNK) <= (qp // CHUNK)
            s = jnp.where(keep, s, NEG)
        m_prev = m_sc[...]
        m_new = jnp.maximum(m_prev, jnp.max(s, axis=-1, keepdims=True))
        alpha = jnp.exp(m_prev - m_new)
        p = jnp.exp(s - m_new)
        l_sc[...] = alpha * l_sc[...] + jnp.sum(p, axis=-1, keepdims=True)
        acc_sc[...] = alpha * acc_sc[...] + jnp.dot(p.astype(v.dtype), v,
                                                    preferred_element_type=F32)
        m_sc[...] = m_new

    def body(j, carry):
        block(j, False)
        return carry

    lax.fori_loop(0, i, body, 0)
    block(i, True)
    o_ref[...] = (acc_sc[...] / l_sc[...]).astype(o_ref.dtype)


def _flash_prompt(q, k, v, heads, dk, cum=None, t=512):
    b, tt, _ = q.shape
    t = _tile(tt, t)
    fox = cum is not None
    in_specs = [pl.BlockSpec((None, t, dk), lambda b_, h, i: (b_, i, h)),
                pl.BlockSpec((None, tt, dk), lambda b_, h, i: (b_, 0, h)),
                pl.BlockSpec((None, tt, HEAD), lambda b_, h, i: (b_, 0, h))]
    args = [q, k, v]
    if fox:
        in_specs += [pl.BlockSpec((None, None, t, 1), lambda b_, h, i: (b_, h, i, 0)),
                     pl.BlockSpec((None, None, 1, tt), lambda b_, h, i: (b_, h, 0, 0))]
        args += [cum.reshape(b, heads, tt, 1), cum.reshape(b, heads, 1, tt)]
    return pl.pallas_call(
        functools.partial(_flash_prompt_kernel, t=t, fox=fox),
        grid=(b, heads, tt // t),
        in_specs=in_specs,
        out_specs=pl.BlockSpec((None, t, HEAD), lambda b_, h, i: (b_, i, h)),
        out_shape=jax.ShapeDtypeStruct((b, tt, heads * HEAD), BF),
        scratch_shapes=[pltpu.VMEM((t, 1), F32), pltpu.VMEM((t, 1), F32),
                        pltpu.VMEM((t, HEAD), F32)],
        compiler_params=_params(("parallel", "parallel", "arbitrary")),
        name="fox_prompt_attn" if fox else "mla_prompt_attn",
    )(*args)


def _mla_sample_kernel(ql_ref, qr_ref, cc_ref, cr_ref, cn_ref, rn_ref, o_ref, m_sc, l_sc, acc_sc):
    j = pl.program_id(1)

    @pl.when(j == 0)
    def _():
        m_sc[...] = jnp.full(m_sc.shape, NEG, F32)
        l_sc[...] = jnp.zeros(l_sc.shape, F32)
        acc_sc[...] = jnp.zeros(acc_sc.shape, F32)

    ql = ql_ref[...]
    qr = qr_ref[...]

    def update(c, r):
        s = (lax.dot_general(ql, c, (((1,), (1,)), ((), ())), preferred_element_type=F32)
             + lax.dot_general(qr, r, (((1,), (1,)), ((), ())), preferred_element_type=F32))
        m_prev = m_sc[...]
        m_new = jnp.maximum(m_prev, jnp.max(s, axis=-1, keepdims=True))
        alpha = jnp.exp(m_prev - m_new)
        p = jnp.exp(s - m_new)
        l_sc[...] = alpha * l_sc[...] + jnp.sum(p, axis=-1, keepdims=True)
        acc_sc[...] = alpha * acc_sc[...] + jnp.dot(p.astype(c.dtype), c,
                                                    preferred_element_type=F32)
        m_sc[...] = m_new

    update(cc_ref[...].astype(BF), cr_ref[...].astype(BF))

    @pl.when(j == pl.num_programs(1) - 1)
    def _():
        update(cn_ref[...], rn_ref[...])
        o_ref[...] = (acc_sc[...] / l_sc[...]).astype(o_ref.dtype)


def _mla_sample(q_lat, q_rope, cache_lat, cache_rope, c_new, r_new, tk=512):
    b, r, c = q_lat.shape
    p = cache_lat.shape[1]
    tn = c_new.shape[1]
    tk = _tile(p, tk)
    return pl.pallas_call(
        _mla_sample_kernel,
        grid=(b, p // tk),
        in_specs=[pl.BlockSpec((None, r, c), lambda b_, j: (b_, 0, 0)),
                  pl.BlockSpec((None, r, LANE), lambda b_, j: (b_, 0, 0)),
                  pl.BlockSpec((None, tk, c), lambda b_, j: (b_, j, 0)),
                  pl.BlockSpec((None, tk, LANE), lambda b_, j: (b_, j, 0)),
                  pl.BlockSpec((None, tn, c), lambda b_, j: (b_, 0, 0)),
                  pl.BlockSpec((None, tn, LANE), lambda b_, j: (b_, 0, 0))],
        out_specs=pl.BlockSpec((None, r, c), lambda b_, j: (b_, 0, 0)),
        out_shape=jax.ShapeDtypeStruct((b, r, c), BF),
        scratch_shapes=[pltpu.VMEM((r, 1), F32), pltpu.VMEM((r, 1), F32),
                        pltpu.VMEM((r, c), F32)],
        compiler_params=_params(("parallel", "arbitrary")),
        name="mla_sample_attn",
    )(q_lat, q_rope, cache_lat, cache_rope, c_new, r_new)


def _fox_sample_kernel(q_ref, ck_ref, cv_ref, kn_ref, vn_ref, cq_ref, cc_ref, cn_ref,
                       o_ref, m_sc, l_sc, acc_sc, *, heads):
    j = pl.program_id(1)

    @pl.when(j == 0)
    def _():
        m_sc[...] = jnp.full(m_sc.shape, NEG, F32)
        l_sc[...] = jnp.zeros(l_sc.shape, F32)
        acc_sc[...] = jnp.zeros(acc_sc.shape, F32)

    def update(h, k, v, bias, causal):
        sl = slice(h * HEAD, (h + 1) * HEAD)
        s = lax.dot_general(q_ref[:, sl], k, (((1,), (1,)), ((), ())),
                            preferred_element_type=F32) + bias
        if causal:
            qp = lax.broadcasted_iota(jnp.int32, s.shape, 0)
            kp = lax.broadcasted_iota(jnp.int32, s.shape, 1)
            s = jnp.where(kp <= qp, s, NEG)
        m_prev = m_sc[h]
        m_new = jnp.maximum(m_prev, jnp.max(s, axis=-1, keepdims=True))
        alpha = jnp.exp(m_prev - m_new)
        p = jnp.exp(s - m_new)
        l_sc[h] = alpha * l_sc[h] + jnp.sum(p, axis=-1, keepdims=True)
        acc_sc[:, sl] = alpha * acc_sc[:, sl] + jnp.dot(p.astype(BF), v,
                                                        preferred_element_type=F32)
        m_sc[h] = m_new

    for h in range(heads):
        sl = slice(h * HEAD, (h + 1) * HEAD)
        bias = cq_ref[h] - cc_ref[h]
        update(h, ck_ref[:, sl].astype(BF), cv_ref[:, sl].astype(BF), bias, False)

    @pl.when(j == pl.num_programs(1) - 1)
    def _():
        for h in range(heads):
            sl = slice(h * HEAD, (h + 1) * HEAD)
            bias = cq_ref[h] - cn_ref[h]
            update(h, kn_ref[:, sl], vn_ref[:, sl], bias, True)
            o_ref[:, sl] = (acc_sc[:, sl] / l_sc[h]).astype(o_ref.dtype)


def _fox_sample(fq, fk_new, fv_new, cache_k, cache_v, cum_q, cum_cache, cum_new, heads, tk=512):
    b, t, d = fq.shape
    p = cache_k.shape[1]
    tk = _tile(p, tk)
    return pl.pallas_call(
        functools.partial(_fox_sample_kernel, heads=heads),
        grid=(b, p // tk),
        in_specs=[pl.BlockSpec((None, t, d), lambda b_, j: (b_, 0, 0)),
                  pl.BlockSpec((None, tk, d), lambda b_, j: (b_, j, 0)),
                  pl.BlockSpec((None, tk, d), lambda b_, j: (b_, j, 0)),
                  pl.BlockSpec((None, t, d), lambda b_, j: (b_, 0, 0)),
                  pl.BlockSpec((None, t, d), lambda b_, j: (b_, 0, 0)),
                  pl.BlockSpec((None, heads, t, 1), lambda b_, j: (b_, 0, 0, 0)),
                  pl.BlockSpec((None, heads, 1, tk), lambda b_, j: (b_, 0, 0, j)),
                  pl.BlockSpec((None, heads, 1, t), lambda b_, j: (b_, 0, 0, 0))],
        out_specs=pl.BlockSpec((None, t, d), lambda b_, j: (b_, 0, 0)),
        out_shape=jax.ShapeDtypeStruct((b, t, d), BF),
        scratch_shapes=[pltpu.VMEM((heads, t, 1), F32), pltpu.VMEM((heads, t, 1), F32),
                        pltpu.VMEM((t, d), F32)],
        compiler_params=_params(("parallel", "arbitrary")),
        name="fox_sample_attn",
    )(fq, cache_k, cache_v, fk_new, fv_new, cum_q, cum_cache, cum_new)


def _out_proj_kernel(x_ref, oa_ref, ob_ref, ga_ref, gb_ref, w_ref, o_ref):
    mix = (ga_ref[...].astype(F32) * oa_ref[...].astype(F32)
           + gb_ref[...].astype(F32) * ob_ref[...].astype(F32)).astype(BF)
    o_ref[...] = x_ref[...] + jnp.dot(mix, w_ref[...], preferred_element_type=F32)


def _out_proj(x, o_a, o_b, gates, w_o, tm=256):
    m, d = x.shape
    tm = _tile(m, tm)
    row = lambda i: (i, 0)
    return pl.pallas_call(
        _out_proj_kernel,
        grid=(m // tm,),
        in_specs=[pl.BlockSpec((tm, d), row), pl.BlockSpec((tm, d), row),
                  pl.BlockSpec((tm, d), row),
                  pl.BlockSpec((tm, d), lambda i: (i, 0)),
                  pl.BlockSpec((tm, d), lambda i: (i, 1)),
                  pl.BlockSpec((d, d), lambda i: (0, 0))],
        out_specs=pl.BlockSpec((tm, d), row),
        out_shape=jax.ShapeDtypeStruct((m, d), F32),
        compiler_params=_params(("parallel",)),
        name="gated_out_proj",
    )(x, o_a, o_b, gates, gates, w_o)


def _moe_up_kernel(be_ref, x_ref, wg_ref, wu_ref, bg_ref, bu_ref, o_ref):
    x = x_ref[...]
    g = jnp.dot(x, wg_ref[...], preferred_element_type=F32) + bg_ref[...]
    u = jnp.dot(x, wu_ref[...], preferred_element_type=F32) + bu_ref[...]
    g = jnp.minimum(g, SWIGLU_LIMIT)
    u = jnp.clip(u, -SWIGLU_LIMIT, SWIGLU_LIMIT)
    o_ref[...] = (g * _sigmoid(SWIGLU_ALPHA * g) * (u + 1.0)).astype(o_ref.dtype)


def _moe_down_kernel(be_ref, a_ref, w_ref, b_ref, rw_ref, o_ref):
    y = jnp.dot(a_ref[...], w_ref[...], preferred_element_type=F32) + b_ref[...]
    o_ref[...] = y * rw_ref[...]


def _moe_experts(xs, block_exp, row_w, w_gu, b_gu, w_down, b_down, bm, tf=512, tn=512):
    p, d = xs.shape
    e, _, ff2 = w_gu.shape
    ff = ff2 // 2
    tf = _tile(ff, tf)
    tn = _tile(d, tn)
    nf = ff // tf
    nb = p // bm
    act = pl.pallas_call(
        _moe_up_kernel,
        grid_spec=pltpu.PrefetchScalarGridSpec(
            num_scalar_prefetch=1,
            grid=(nf, nb),
            in_specs=[pl.BlockSpec((bm, d), lambda f, i, be: (i, 0)),
                      pl.BlockSpec((None, d, tf), lambda f, i, be: (be[i], 0, f)),
                      pl.BlockSpec((None, d, tf), lambda f, i, be: (be[i], 0, nf + f)),
                      pl.BlockSpec((None, 1, tf), lambda f, i, be: (be[i], 0, f)),
                      pl.BlockSpec((None, 1, tf), lambda f, i, be: (be[i], 0, nf + f))],
            out_specs=pl.BlockSpec((bm, tf), lambda f, i, be: (i, f)),
        ),
        out_shape=jax.ShapeDtypeStruct((p, ff), BF),
        compiler_params=_params(("parallel", "arbitrary")),
        name="moe_gate_up",
    )(block_exp, xs, w_gu, w_gu, b_gu.reshape(e, 1, ff2), b_gu.reshape(e, 1, ff2))
    return pl.pallas_call(
        _moe_down_kernel,
        grid_spec=pltpu.PrefetchScalarGridSpec(
            num_scalar_prefetch=1,
            grid=(d // tn, nb),
            in_specs=[pl.BlockSpec((bm, ff), lambda n, i, be: (i, 0)),
                      pl.BlockSpec((None, ff, tn), lambda n, i, be: (be[i], 0, n)),
                      pl.BlockSpec((None, 1, tn), lambda n, i, be: (be[i], 0, n)),
                      pl.BlockSpec((bm, 1), lambda n, i, be: (i, 0))],
            out_specs=pl.BlockSpec((bm, tn), lambda n, i, be: (i, n)),
        ),
        out_shape=jax.ShapeDtypeStruct((p, d), F32),
        compiler_params=_params(("parallel", "arbitrary")),
        name="moe_down",
    )(block_exp, act, w_down, b_down.reshape(e, 1, d), row_w.reshape(p, 1))


def _moe(xn, logits, w_gu, b_gu, w_down, b_down, bm):
    n, d = xn.shape
    e = w_gu.shape[0]
    top_v, top_i = lax.top_k(logits, TOP_K)
    gates = jax.nn.softmax(top_v, axis=-1)
    nk = n * TOP_K
    a_exp = top_i.reshape(-1).astype(jnp.int32)
    onehot = (a_exp[:, None] == jnp.arange(e, dtype=jnp.int32)[None, :]).astype(jnp.int32)
    rank = jnp.sum((jnp.cumsum(onehot, axis=0) - onehot) * onehot, axis=1)
    counts = jnp.sum(onehot, axis=0)
    padded = (counts + bm - 1) // bm * bm
    pends = jnp.cumsum(padded)
    pstarts = pends - padded
    dest = pstarts[a_exp] + rank
    nb = -(-(nk + e * (bm - 1)) // bm)
    p = nb * bm
    a_tok = jnp.arange(nk, dtype=jnp.int32) // TOP_K
    buf_tok = jnp.full((p,), n, jnp.int32).at[dest].set(a_tok)
    buf_w = jnp.zeros((p,), F32).at[dest].set(gates.reshape(-1))
    block_exp = jnp.clip(jnp.searchsorted(pends, jnp.arange(nb, dtype=jnp.int32) * bm, side='right'),
                         0, e - 1).astype(jnp.int32)
    xpad = jnp.concatenate([xn, jnp.zeros((1, d), xn.dtype)], axis=0)
    xs = jnp.take(xpad, buf_tok, axis=0)
    out = _moe_experts(xs, block_exp, buf_w, w_gu, b_gu, w_down, b_down, bm)
    picked = jnp.take(out, dest.reshape(n, TOP_K), axis=0)
    return jnp.sum(picked, axis=1)


def _ple_kernel(x_ref, y_ref, p_ref, wg_ref, bg_ref, wp_ref, gf_ref, o_ref, *, final):
    x = x_ref[...] + y_ref[...]
    gate = _sigmoid(jnp.dot(x.astype(BF), wg_ref[...], preferred_element_type=F32) + bg_ref[...])
    x = x + gate * jnp.dot(p_ref[...].astype(BF), wp_ref[...], preferred_element_type=F32)
    if final:
        x = _rms(x, gf_ref[...])
    o_ref[...] = x


def _ple(x, y_moe, p_emb, w_plg, b_plg, w_pl, g_final, final, tm=256):
    m, d = x.shape
    pd = p_emb.shape[1]
    tm = _tile(m, tm)
    row = lambda i: (i, 0)
    fixed = lambda i: (0, 0)
    return pl.pallas_call(
        functools.partial(_ple_kernel, final=final),
        grid=(m // tm,),
        in_specs=[pl.BlockSpec((tm, d), row), pl.BlockSpec((tm, d), row),
                  pl.BlockSpec((tm, pd), row),
                  pl.BlockSpec((d, d), fixed), pl.BlockSpec((1, d), fixed),
                  pl.BlockSpec((pd, d), fixed), pl.BlockSpec((1, d), fixed)],
        out_specs=pl.BlockSpec((tm, d), row),
        out_shape=jax.ShapeDtypeStruct((m, d), F32),
        compiler_params=_params(("parallel",)),
        name="ple_gate_final",
    )(x, y_moe, p_emb, w_plg, b_plg.reshape(1, d), w_pl, g_final.reshape(1, d))


def _rot_cols(w):
    half = w.shape[-1] // 2
    return jnp.concatenate([-w[..., half:], w[..., :half]], axis=-1)


def _rope_table(pos):
    half = ROPE // 2
    freqs = ROPE_THETA ** (-jnp.arange(half, dtype=F32) / half)
    ang = pos.astype(F32)[:, None] * freqs[None, :]
    cos, sin = jnp.cos(ang), jnp.sin(ang)
    return jnp.concatenate([cos, cos, sin, sin], axis=-1)


def _layer_weights(w_in, b_forget, b_gate, w_q_up, w_uk, w_uv, heads, ql, kvl):
    d = w_in.shape[0]
    o = 0
    parts = []
    for s in (ql, kvl, ROPE, d, d, d, heads, 2 * d):
        parts.append(w_in[:, o:o + s])
        o += s
    w_cq, w_ckv, w_kr, w_fq, w_fk, w_fv, w_fg, w_gate = parts
    w_lat = jnp.concatenate(
        [w_cq, w_ckv, w_kr, _rot_cols(w_kr), jnp.pad(w_fg, ((0, 0), (0, LANE - heads)))], axis=1)
    wq_nope = w_q_up[:, :, :HEAD]
    wq_rope = w_q_up[:, :, HEAD:]
    wq = jnp.concatenate([wq_nope, wq_rope, _rot_cols(wq_rope)], axis=-1).reshape(ql, heads * QK_PAD)
    return dict(
        w_lat=w_lat.astype(BF), w_fq=w_fq.astype(BF), w_fk=w_fk.astype(BF), w_fv=w_fv.astype(BF),
        w_gate=w_gate.astype(BF), wq=wq.astype(BF),
        w_uk=w_uk.reshape(kvl, heads * HEAD).astype(BF),
        w_uv=w_uv.reshape(kvl, heads * HEAD).astype(BF),
        w_uk_t=jnp.transpose(w_uk, (1, 2, 0)).astype(BF),
        w_uv_h=jnp.transpose(w_uv, (1, 0, 2)).astype(BF),
        b_forget=jnp.pad(b_forget, (0, LANE - heads)).reshape(1, LANE),
        b_gate=b_gate.reshape(1, 2 * d))


def _mixer_inputs(x, cs, lw, g_mix, g_q_lat, g_kv_lat, heads, ql, kvl, tm):
    n, d = x.shape
    h = _rmsnorm(x, g_mix, BF)
    tm = _tile(n, tm)
    row = lambda i, j: (i, 0)
    fixed = lambda i, j: (0, 0)
    wl = ql + kvl + 2 * LANE
    cqn, clat, clatb, kr, krb, logf = _mm(
        h, lw['w_lat'], functools.partial(_latent_epilogue, ql=ql, kvl=kvl),
        [jax.ShapeDtypeStruct((n, ql), BF), jax.ShapeDtypeStruct((n, kvl), F32),
         jax.ShapeDtypeStruct((n, kvl), BF), jax.ShapeDtypeStruct((n, LANE), F32),
         jax.ShapeDtypeStruct((n, LANE), BF), jax.ShapeDtypeStruct((n, LANE), F32)],
        [ql, kvl, kvl, LANE, LANE, LANE], tm=tm, tn=wl,
        extra=(g_q_lat.reshape(1, ql), g_kv_lat.reshape(1, kvl), lw['b_forget'], cs),
        extra_specs=(pl.BlockSpec((1, ql), fixed), pl.BlockSpec((1, kvl), fixed),
                     pl.BlockSpec((1, LANE), fixed), pl.BlockSpec((tm, LANE), row)),
        name="latent_proj")
    tn = _tile(d, 512)
    (fq,) = _mm(h, lw['w_fq'], functools.partial(_scaled_bf16_epilogue, scale=FOX_SCALE),
                [jax.ShapeDtypeStruct((n, d), BF)], [tn], tm=tm, tn=tn, name="fox_q_proj")
    fk, fkb = _mm(h, lw['w_fk'], _dual_epilogue,
                  [jax.ShapeDtypeStruct((n, d), F32), jax.ShapeDtypeStruct((n, d), BF)],
                  [tn, tn], tm=tm, tn=tn, name="fox_k_proj")
    fv, fvb = _mm(h, lw['w_fv'], _dual_epilogue,
                  [jax.ShapeDtypeStruct((n, d), F32), jax.ShapeDtypeStruct((n, d), BF)],
                  [tn, tn], tm=tm, tn=tn, name="fox_v_proj")
    (gates,) = _mm(h, lw['w_gate'], _gate_epilogue, [jax.ShapeDtypeStruct((n, 2 * d), BF)], [tn],
                   tm=tm, tn=tn, extra=(lw['b_gate'],),
                   extra_specs=(pl.BlockSpec((1, tn), lambda i, j: (0, j)),), name="merge_gates")
    hpt = 2 if heads % 2 == 0 else 1
    (q,) = _mm(cqn, lw['wq'], functools.partial(_q_epilogue, heads_per_tile=hpt),
               [jax.ShapeDtypeStruct((n, heads * QK_PAD), BF)], [hpt * QK_PAD],
               tm=tm, tn=hpt * QK_PAD, extra=(cs,),
               extra_specs=(pl.BlockSpec((tm, LANE), row),), name="mla_q_up")
    return dict(q=q, clat=clat, clatb=clatb, kr=kr, krb=krb, logf=logf, fq=fq, fk=fk, fkb=fkb,
                fv=fv, fvb=fvb, gates=gates)


def _prompt_attention(mi, lw, b, t, heads, kvl, tm):
    n = b * t
    d = heads * HEAD
    hpt = 2 if heads % 2 == 0 else 1
    tm = _tile(n, tm)
    (k,) = _mm(mi['clatb'], lw['w_uk'], functools.partial(_k_epilogue, heads_per_tile=hpt),
               [jax.ShapeDtypeStruct((n, heads * QK_PAD), BF)], [hpt * QK_PAD],
               tm=tm, tn=hpt * HEAD, extra=(mi['krb'],),
               extra_specs=(pl.BlockSpec((tm, LANE), lambda i, j: (i, 0)),), name="mla_k_up")
    (v,) = _mm(mi['clatb'], lw['w_uv'], _bf16_epilogue, [jax.ShapeDtypeStruct((n, d), BF)],
               [_tile(d, 512)], tm=tm, tn=_tile(d, 512), name="mla_v_up")
    o_a = _flash_prompt(mi['q'].reshape(b, t, -1), k.reshape(b, t, -1), v.reshape(b, t, -1),
                        heads, QK_PAD)
    logf = mi['logf'][:, :heads].reshape(b, t, heads)
    cum = jnp.cumsum(logf, axis=1).transpose(0, 2, 1)
    o_b = _flash_prompt(mi['fq'].reshape(b, t, -1), mi['fkb'].reshape(b, t, -1),
                        mi['fvb'].reshape(b, t, -1), heads, HEAD, cum=cum)
    return o_a.reshape(n, d), o_b.reshape(n, d)


def _sample_attention(mi, lw, caches, b, t, heads, kvl, tm):
    cache_lat, cache_rope, cache_k, cache_v, cache_logf = caches
    n = b * t
    d = heads * HEAD
    p = cache_lat.shape[1]
    tm = _tile(n, tm)
    (q_lat,) = _mm(mi['q'], lw['w_uk_t'], _bf16_epilogue,
                   [jax.ShapeDtypeStruct((n, heads * kvl), BF)], [kvl], tm=tm, tn=kvl,
                   a_spec=pl.BlockSpec((tm, HEAD), lambda i, j: (i, 2 * j)),
                   w_spec=pl.BlockSpec((None, HEAD, kvl), lambda i, j: (j, 0, 0)),
                   n_col_tiles=heads, name="mla_q_absorb")
    q_rope = mi['q'].reshape(n, heads, QK_PAD)[:, :, HEAD:]
    cache_rope_p = jnp.pad(cache_rope, ((0, 0), (0, 0), (0, LANE - ROPE)))
    o_lat = _mla_sample(q_lat.reshape(b, t * heads, kvl), q_rope.reshape(b, t * heads, LANE),
                        cache_lat, cache_rope_p, mi['clatb'].reshape(b, t, kvl),
                        mi['krb'].reshape(b, t, LANE))
    (o_a,) = _mm(o_lat.reshape(n, heads * kvl), lw['w_uv_h'], _bf16_epilogue,
                 [jax.ShapeDtypeStruct((n, d), BF)], [HEAD], tm=tm, tn=HEAD,
                 a_spec=pl.BlockSpec((tm, kvl), lambda i, j: (i, j)),
                 w_spec=pl.BlockSpec((None, kvl, HEAD), lambda i, j: (j, 0, 0)),
                 n_col_tiles=heads, name="mla_v_absorb")
    logf = mi['logf'][:, :heads].reshape(b, t, heads)
    cum = jnp.cumsum(jnp.concatenate([cache_logf.astype(F32), logf], axis=1), axis=1)
    cum = cum.transpose(0, 2, 1)
    o_b = _fox_sample(mi['fq'].reshape(b, t, d), mi['fkb'].reshape(b, t, d),
                      mi['fvb'].reshape(b, t, d), cache_k.reshape(b, p, d),
                      cache_v.reshape(b, p, d), cum[:, :, p:, None], cum[:, :, None, :p],
                      cum[:, :, None, p:], heads)
    return o_a, o_b.reshape(n, d)


def kernel(x_prompt, x_sample, p_prompt, p_sample, cache_mla_latent, cache_mla_rope, cache_fox_k,
           cache_fox_v, cache_fox_logf, g_mix, w_in, b_forget, b_gate, g_q_lat, w_q_up, g_kv_lat,
           w_uk, w_uv, w_o, g_ffn, w_router, b_router, w_gu, b_gu, w_down, b_down, w_pl, w_plg,
           b_plg, g_final):
    bp, tp, d = x_prompt.shape
    bs, ts, _ = x_sample.shape
    depth = g_mix.shape[0]
    heads = d // HEAD
    ql = w_q_up.shape[1]
    kvl = w_uk.shape[1]
    past = cache_mla_latent.shape[2]
    n_p, n_s = bp * tp, bs * ts
    tm = 1024
    bm = 256

    cs_p = jnp.tile(_rope_table(jnp.arange(tp)), (bp, 1))
    cs_s = jnp.tile(_rope_table(past + jnp.arange(ts)), (bs, 1))
    x_p = x_prompt.reshape(n_p, d)
    x_s = x_sample.reshape(n_s, d)
    st_p, st_s = [], []
    for i in range(depth):
        last = i == depth - 1
        lw = _layer_weights(w_in[i], b_forget[i], b_gate[i], w_q_up[i], w_uk[i], w_uv[i],
                            heads, ql, kvl)
        w_o_b = w_o[i].astype(BF)
        mi_p = _mixer_inputs(x_p, cs_p, lw, g_mix[i], g_q_lat[i], g_kv_lat[i], heads, ql, kvl, tm)
        mi_s = _mixer_inputs(x_s, cs_s, lw, g_mix[i], g_q_lat[i], g_kv_lat[i], heads, ql, kvl, tm)
        oa_p, ob_p = _prompt_attention(mi_p, lw, bp, tp, heads, kvl, tm)
        caches = (cache_mla_latent[i], cache_mla_rope[i], cache_fox_k[i], cache_fox_v[i],
                  cache_fox_logf[i])
        oa_s, ob_s = _sample_attention(mi_s, lw, caches, bs, ts, heads, kvl, tm)
        x_p = _out_proj(x_p, oa_p, ob_p, mi_p['gates'], w_o_b)
        x_s = _out_proj(x_s, oa_s, ob_s, mi_s['gates'], w_o_b)
        xn_p, lg_p = _ffn_norm(x_p, g_ffn[i], w_router[i], b_router[i])
        xn_s, lg_s = _ffn_norm(x_s, g_ffn[i], w_router[i], b_router[i])
        y = _moe(jnp.concatenate([xn_p, xn_s], axis=0), jnp.concatenate([lg_p, lg_s], axis=0),
                 w_gu[i].astype(BF), b_gu[i], w_down[i].astype(BF), b_down[i], bm)
        w_plg_b = w_plg[i].astype(BF)
        w_pl_b = w_pl[i].astype(BF)
        x_p = _ple(x_p, y[:n_p], p_prompt[i].reshape(n_p, -1), w_plg_b, b_plg[i], w_pl_b,
                   g_final, last)
        x_s = _ple(x_s, y[n_p:], p_sample[i].reshape(n_s, -1), w_plg_b, b_plg[i], w_pl_b,
                   g_final, last)
        for st, mi, b, t in ((st_p, mi_p, bp, tp), (st_s, mi_s, bs, ts)):
            st.append((mi['clat'].reshape(b, t, kvl),
                       mi['kr'][:, :ROPE].reshape(b, t, ROPE),
                       mi['fk'].reshape(b, t, heads, HEAD),
                       mi['fv'].reshape(b, t, heads, HEAD),
                       mi['logf'][:, :heads].reshape(b, t, heads)))
    if depth == 0:
        x_p = _rmsnorm(x_p, g_final, F32)
        x_s = _rmsnorm(x_s, g_final, F32)
    outs = [x_p.reshape(bp, tp, d), x_s.reshape(bs, ts, d)]
    for st in (st_p, st_s):
        for k in range(5):
            outs.append(jnp.stack([s[k] for s in st]))
    return tuple(outs)
```

```python
import functools

import jax
import jax.numpy as jnp
from jax import lax
from jax.experimental import pallas as pl
from jax.experimental.pallas import tpu as pltpu

CHUNK = 64
NORM_EPS = 1e-6
HEAD = 128
ROPE = 64
QK_PAD = 256
MLA_SCALE = (HEAD + ROPE) ** -0.5
FOX_SCALE = HEAD ** -0.5
ROPE_THETA = 10000.0
TOP_K = 4
SWIGLU_LIMIT = 7.0
SWIGLU_ALPHA = 1.702
LANE = 128
VMEM_LIMIT = 56 * 1024 * 1024
NEG = -1e30
LOG2E = 1.4426950408889634
FLASH_BLOCK = 1024

BF = jnp.bfloat16
F32 = jnp.float32


def _tile(n, pref):
    if n <= pref:
        return n
    for t in range(pref, 7, -1):
        if n % t == 0 and t % 8 == 0:
            return t
    return n


def _params(sem):
    return pltpu.CompilerParams(dimension_semantics=sem, vmem_limit_bytes=VMEM_LIMIT)


def _rms(x, g):
    return x * lax.rsqrt(jnp.mean(x * x, axis=-1, keepdims=True) + NORM_EPS) * g


def _sigmoid(x):
    return 1.0 / (1.0 + jnp.exp(-x))


def _lanes(x, n):
    return jnp.tile(x, (1, n // LANE)) if n >= LANE else x[:, :n]


def _rmsnorm_kernel(x_ref, g_ref, o_ref):
    o_ref[...] = _rms(x_ref[...], g_ref[...]).astype(o_ref.dtype)


def _rmsnorm(x, g, out_dtype, tm=512):
    m, d = x.shape
    tm = _tile(m, tm)
    return pl.pallas_call(
        _rmsnorm_kernel,
        grid=(m // tm,),
        in_specs=[pl.BlockSpec((tm, d), lambda i: (i, 0)),
                  pl.BlockSpec((1, d), lambda i: (0, 0))],
        out_specs=pl.BlockSpec((tm, d), lambda i: (i, 0)),
        out_shape=jax.ShapeDtypeStruct((m, d), out_dtype),
        compiler_params=_params(("parallel",)),
        name="rmsnorm",
    )(x, g.reshape(1, d))


def _ffn_norm_kernel(x_ref, g_ref, wr_ref, br_ref, xn_ref, lg_ref):
    xn = _rms(x_ref[...], g_ref[...])
    xn_ref[...] = xn.astype(xn_ref.dtype)
    lg_ref[...] = jnp.dot(xn, wr_ref[...], preferred_element_type=F32,
                          precision=lax.Precision.HIGHEST) + br_ref[...]


def _ffn_norm(x, g, w_router, b_router, tm=512):
    m, d = x.shape
    e = w_router.shape[1]
    ep = -(-e // LANE) * LANE
    wr = jnp.pad(w_router, ((0, 0), (0, ep - e)))
    br = jnp.pad(b_router, (0, ep - e)).reshape(1, ep)
    tm = _tile(m, tm)
    xn, lg = pl.pallas_call(
        _ffn_norm_kernel,
        grid=(m // tm,),
        in_specs=[pl.BlockSpec((tm, d), lambda i: (i, 0)),
                  pl.BlockSpec((1, d), lambda i: (0, 0)),
                  pl.BlockSpec((d, ep), lambda i: (0, 0)),
                  pl.BlockSpec((1, ep), lambda i: (0, 0))],
        out_specs=[pl.BlockSpec((tm, d), lambda i: (i, 0)),
                   pl.BlockSpec((tm, ep), lambda i: (i, 0))],
        out_shape=[jax.ShapeDtypeStruct((m, d), BF),
                   jax.ShapeDtypeStruct((m, ep), F32)],
        compiler_params=_params(("parallel",)),
        name="ffn_norm_router",
    )(x, g.reshape(1, d), wr, br)
    return xn, lg[:, :e]


def _mm_kernel(*refs, n_extra, epilogue):
    a_ref, w_ref = refs[0], refs[1]
    extra = refs[2:2 + n_extra]
    outs = refs[2 + n_extra:]
    acc = jnp.dot(a_ref[...], w_ref[...], preferred_element_type=F32)
    epilogue(acc, extra, outs)


def _mm(a, w, epilogue, out_shapes, out_widths, *, tm, tn, extra=(), extra_specs=(),
        a_spec=None, w_spec=None, n_col_tiles=None, name="mm"):
    m = a.shape[0]
    k = a.shape[-1] if a_spec is None else None
    tm = _tile(m, tm)
    if n_col_tiles is None:
        n_col_tiles = w.shape[-1] // tn
    if a_spec is None:
        a_spec = pl.BlockSpec((tm, k), lambda i, j: (i, 0))
    if w_spec is None:
        w_spec = pl.BlockSpec((w.shape[0], tn), lambda i, j: (0, j))
    out_specs = [pl.BlockSpec((tm, ow), lambda i, j: (i, j)) for ow in out_widths]
    return pl.pallas_call(
        functools.partial(_mm_kernel, n_extra=len(extra), epilogue=epilogue),
        grid=(m // tm, n_col_tiles),
        in_specs=[a_spec, w_spec, *extra_specs],
        out_specs=out_specs,
        out_shape=out_shapes,
        compiler_params=_params(("parallel", "arbitrary")),
        name=name,
    )(a, w, *extra)


def _rope_mix(v, cs_ref):
    cs = cs_ref[...]
    lane = lax.broadcasted_iota(jnp.int32, cs.shape, 1)
    c = jnp.where(lane < ROPE, cs, 0.0)
    s = jnp.where(lane < ROPE, pltpu.roll(cs, ROPE, 1), 0.0)
    return v * c + pltpu.roll(v, ROPE, 1) * s


def _latent_epilogue(acc, extra, outs, *, ql, kvl):
    gq_ref, gkv_ref, bf_ref, cs_ref = extra
    cqn_ref, clat_ref, clatb_ref, kr_ref, krb_ref, logf_ref = outs
    cqn_ref[...] = _rms(acc[:, :ql], gq_ref[...]).astype(cqn_ref.dtype)
    clat = _rms(acc[:, ql:ql + kvl], gkv_ref[...])
    clat_ref[...] = clat
    clatb_ref[...] = clat.astype(clatb_ref.dtype)
    o = ql + kvl
    kr = _rope_mix(acc[:, o:o + LANE], cs_ref)
    kr_ref[...] = kr
    krb_ref[...] = kr.astype(krb_ref.dtype)
    fg = acc[:, o + LANE:o + 2 * LANE] + bf_ref[...]
    logf_ref[...] = jnp.minimum(fg, 0.0) - jnp.log(1.0 + jnp.exp(-jnp.abs(fg)))


def _scaled_bf16_epilogue(acc, extra, outs, *, scale):
    outs[0][...] = (acc * scale).astype(outs[0].dtype)


def _dual_epilogue(acc, extra, outs):
    outs[0][...] = acc
    outs[1][...] = acc.astype(outs[1].dtype)


def _gate_epilogue(acc, extra, outs):
    outs[0][...] = _sigmoid(acc + extra[0][...]).astype(outs[0].dtype)


def _q_epilogue(acc, extra, outs, *, heads_per_tile, scale):
    cs_ref = extra[0]
    o_ref = outs[0]
    for h in range(heads_per_tile):
        b = h * QK_PAD
        o_ref[:, b:b + HEAD] = (acc[:, b:b + HEAD] * scale).astype(o_ref.dtype)
        o_ref[:, b + HEAD:b + QK_PAD] = (
            _rope_mix(acc[:, b + HEAD:b + QK_PAD], cs_ref) * scale).astype(o_ref.dtype)


def _fox_q_aug_epilogue(acc, extra, outs, *, heads_per_tile, scale):
    o_ref = outs[0]
    lane = lax.broadcasted_iota(jnp.int32, (acc.shape[0], HEAD), 1)
    ones = jnp.where(lane < 3, 1.0, 0.0).astype(o_ref.dtype)
    for h in range(heads_per_tile):
        o_ref[:, h * QK_PAD:h * QK_PAD + HEAD] = (
            acc[:, h * HEAD:(h + 1) * HEAD] * scale).astype(o_ref.dtype)
        o_ref[:, h * QK_PAD + HEAD:(h + 1) * QK_PAD] = ones


def _fox_k_aug_epilogue(acc, extra, outs, *, heads_per_tile):
    ext_ref = extra[0]
    k_ref, o_ref = outs
    k_ref[...] = acc
    for h in range(heads_per_tile):
        o_ref[:, h * QK_PAD:h * QK_PAD + HEAD] = acc[:, h * HEAD:(h + 1) * HEAD].astype(o_ref.dtype)
        o_ref[:, h * QK_PAD + HEAD:(h + 1) * QK_PAD] = ext_ref[:, h * HEAD:(h + 1) * HEAD]


def _k_epilogue(acc, extra, outs, *, heads_per_tile):
    krb_ref = extra[0]
    o_ref = outs[0]
    for h in range(heads_per_tile):
        o_ref[:, h * QK_PAD:h * QK_PAD + HEAD] = acc[:, h * HEAD:(h + 1) * HEAD].astype(o_ref.dtype)
        o_ref[:, h * QK_PAD + HEAD:(h + 1) * QK_PAD] = krb_ref[...]


def _bf16_epilogue(acc, extra, outs):
    outs[0][...] = acc.astype(outs[0].dtype)


def _flash_prompt_kernel(q_ref, k_ref, v_ref, o_ref, m_sc, l_sc, acc_sc, *, t, causal):
    i = pl.program_id(2)
    m_sc[...] = jnp.full(m_sc.shape, NEG, F32)
    l_sc[...] = jnp.zeros(l_sc.shape, F32)
    acc_sc[...] = jnp.zeros(acc_sc.shape, F32)
    rep = t // LANE

    def block(j, masked):
        start = pl.multiple_of(j * t, t)
        k = k_ref[pl.ds(start, t), :]
        v = v_ref[pl.ds(start, t), :]
        s = lax.dot_general(q_ref[...], k, (((1,), (1,)), ((), ())), preferred_element_type=F32)
        if masked:
            qp = lax.broadcasted_iota(jnp.int32, s.shape, 0)
            kp = lax.broadcasted_iota(jnp.int32, s.shape, 1)
            keep = (kp <= qp) if causal else ((kp // CHUNK) <= (qp // CHUNK))
            s = jnp.where(keep, s, NEG)
        m_prev = m_sc[...]
        m_new = jnp.maximum(m_prev, jnp.max(s, axis=-1, keepdims=True))
        alpha = jnp.exp2(m_prev - m_new)
        p = jnp.exp2(s - jnp.tile(m_new, (1, rep)))
        l_sc[...] = alpha * l_sc[...] + jnp.sum(p, axis=-1, keepdims=True)
        acc_sc[...] = alpha * acc_sc[...] + jnp.dot(p.astype(v.dtype), v,
                                                    preferred_element_type=F32)
        m_sc[...] = m_new

    def body(j, carry):
        block(j, False)
        return carry

    lax.fori_loop(0, i, body, 0)
    block(i, True)
    o_ref[...] = (acc_sc[...] / l_sc[...]).astype(o_ref.dtype)


def _flash_prompt(q, k, v, heads, causal, t=None):
    b, tt, _ = q.shape
    t = _tile(tt, FLASH_BLOCK if t is None else t)
    return pl.pallas_call(
        functools.partial(_flash_prompt_kernel, t=t, causal=causal),
        grid=(b, heads, tt // t),
        in_specs=[pl.BlockSpec((None, t, QK_PAD), lambda b_, h, i: (b_, i, h)),
                  pl.BlockSpec((None, tt, QK_PAD), lambda b_, h, i: (b_, 0, h)),
                  pl.BlockSpec((None, tt, HEAD), lambda b_, h, i: (b_, 0, h))],
        out_specs=pl.BlockSpec((None, t, HEAD), lambda b_, h, i: (b_, i, h)),
        out_shape=jax.ShapeDtypeStruct((b, tt, heads * HEAD), BF),
        scratch_shapes=[pltpu.VMEM((t, LANE), F32), pltpu.VMEM((t, LANE), F32),
                        pltpu.VMEM((t, HEAD), F32)],
        compiler_params=_params(("parallel", "parallel", "arbitrary")),
        name="fox_prompt_attn" if causal else "mla_prompt_attn",
    )(q, k, v)


def _mla_sample_kernel(ql_ref, qr_ref, cc_ref, cr_ref, cn_ref, rn_ref, o_ref, m_sc, l_sc, acc_sc):
    j = pl.program_id(1)

    @pl.when(j == 0)
    def _():
        m_sc[...] = jnp.full(m_sc.shape, NEG, F32)
        l_sc[...] = jnp.zeros(l_sc.shape, F32)
        acc_sc[...] = jnp.zeros(acc_sc.shape, F32)

    ql = ql_ref[...]
    qr = qr_ref[...]

    def update(c, r):
        s = (lax.dot_general(ql, c, (((1,), (1,)), ((), ())), preferred_element_type=F32)
             + lax.dot_general(qr, r, (((1,), (1,)), ((), ())), preferred_element_type=F32))
        m_prev = m_sc[...]
        m_new = jnp.maximum(m_prev, jnp.max(s, axis=-1, keepdims=True))
        alpha = jnp.exp(m_prev - m_new)
        p = jnp.exp(s - _lanes(m_new, s.shape[1]))
        l_sc[...] = alpha * l_sc[...] + jnp.sum(p, axis=-1, keepdims=True)
        acc_sc[...] = _lanes(alpha, c.shape[1]) * acc_sc[...] + jnp.dot(
            p.astype(c.dtype), c, preferred_element_type=F32)
        m_sc[...] = m_new

    update(cc_ref[...].astype(BF), cr_ref[...].astype(BF))

    @pl.when(j == pl.num_programs(1) - 1)
    def _():
        update(cn_ref[...], rn_ref[...])
        o_ref[...] = (acc_sc[...] / _lanes(l_sc[...], o_ref.shape[1])).astype(o_ref.dtype)


def _mla_sample(q_lat, q_rope, cache_lat, cache_rope, c_new, r_new, tk=512):
    b, r, c = q_lat.shape
    p = cache_lat.shape[1]
    tn = c_new.shape[1]
    tk = _tile(p, tk)
    return pl.pallas_call(
        _mla_sample_kernel,
        grid=(b, p // tk),
        in_specs=[pl.BlockSpec((None, r, c), lambda b_, j: (b_, 0, 0)),
                  pl.BlockSpec((None, r, LANE), lambda b_, j: (b_, 0, 0)),
                  pl.BlockSpec((None, tk, c), lambda b_, j: (b_, j, 0)),
                  pl.BlockSpec((None, tk, LANE), lambda b_, j: (b_, j, 0)),
                  pl.BlockSpec((None, tn, c), lambda b_, j: (b_, 0, 0)),
                  pl.BlockSpec((None, tn, LANE), lambda b_, j: (b_, 0, 0))],
        out_specs=pl.BlockSpec((None, r, c), lambda b_, j: (b_, 0, 0)),
        out_shape=jax.ShapeDtypeStruct((b, r, c), BF),
        scratch_shapes=[pltpu.VMEM((r, LANE), F32), pltpu.VMEM((r, LANE), F32),
                        pltpu.VMEM((r, c), F32)],
        compiler_params=_params(("parallel", "arbitrary")),
        name="mla_sample_attn",
    )(q_lat, q_rope, cache_lat, cache_rope, c_new, r_new)


def _fox_sample_kernel(q_ref, ck_ref, cv_ref, kn_ref, vn_ref, cc_ref, cn_ref,
                       o_ref, m_sc, l_sc, acc_sc, *, heads):
    j = pl.program_id(1)
    t = q_ref.shape[0]
    head_cols = lambda h: slice(h * HEAD, (h + 1) * HEAD)
    head_rows = lambda h: slice(h * t, (h + 1) * t)

    @pl.when(j == 0)
    def _():
        m_sc[...] = jnp.full(m_sc.shape, NEG, F32)
        l_sc[...] = jnp.zeros(l_sc.shape, F32)
        acc_sc[...] = jnp.zeros(acc_sc.shape, F32)

    def update(keys, vals, neg_cum, causal):
        s = jnp.concatenate(
            [lax.dot_general(q_ref[:, head_cols(h)], keys(h), (((1,), (1,)), ((), ())),
                             preferred_element_type=F32) + neg_cum(h) for h in range(heads)],
            axis=0)
        if causal:
            qp = lax.broadcasted_iota(jnp.int32, s.shape, 0) % t
            kp = lax.broadcasted_iota(jnp.int32, s.shape, 1)
            s = jnp.where(kp <= qp, s, NEG)
        m_prev = m_sc[...]
        m_new = jnp.maximum(m_prev, jnp.max(s, axis=-1, keepdims=True))
        alpha = jnp.exp(m_prev - m_new)
        p = jnp.exp(s - _lanes(m_new, s.shape[1]))
        l_sc[...] = alpha * l_sc[...] + jnp.sum(p, axis=-1, keepdims=True)
        p = p.astype(BF)
        pv = jnp.concatenate(
            [jnp.dot(p[head_rows(h)], vals(h), preferred_element_type=F32) for h in range(heads)],
            axis=0)
        acc_sc[...] = alpha * acc_sc[...] + pv
        m_sc[...] = m_new

    tk = ck_ref.shape[0] // heads
    cache_rows = lambda h: pl.ds(h, tk, stride=heads)
    update(lambda h: ck_ref[cache_rows(h), :].astype(BF),
           lambda h: cv_ref[cache_rows(h), :].astype(BF),
           lambda h: cc_ref[h], False)

    @pl.when(j == pl.num_programs(1) - 1)
    def _():
        update(lambda h: kn_ref[:, head_cols(h)], lambda h: vn_ref[:, head_cols(h)],
               lambda h: cn_ref[h], True)
        o = acc_sc[...] / l_sc[...]
        for h in range(heads):
            o_ref[:, head_cols(h)] = o[head_rows(h)].astype(o_ref.dtype)


def _fox_sample(fq, fk_new, fv_new, cache_k, cache_v, layer, neg_cum_cache, neg_cum_new, heads,
                tk=512):
    b, t, d = fq.shape
    p = cache_k.shape[2]
    tk = _tile(p, tk)
    cache_spec = pl.BlockSpec((None, None, tk * heads, HEAD), lambda b_, j: (layer, b_, j, 0))
    flat = lambda c: c.reshape(c.shape[0], b, p * heads, HEAD)
    return pl.pallas_call(
        functools.partial(_fox_sample_kernel, heads=heads),
        grid=(b, p // tk),
        in_specs=[pl.BlockSpec((None, t, d), lambda b_, j: (b_, 0, 0)),
                  cache_spec, cache_spec,
                  pl.BlockSpec((None, t, d), lambda b_, j: (b_, 0, 0)),
                  pl.BlockSpec((None, t, d), lambda b_, j: (b_, 0, 0)),
                  pl.BlockSpec((None, heads, 1, tk), lambda b_, j: (b_, 0, 0, j)),
                  pl.BlockSpec((None, heads, 1, t), lambda b_, j: (b_, 0, 0, 0))],
        out_specs=pl.BlockSpec((None, t, d), lambda b_, j: (b_, 0, 0)),
        out_shape=jax.ShapeDtypeStruct((b, t, d), BF),
        scratch_shapes=[pltpu.VMEM((heads * t, LANE), F32), pltpu.VMEM((heads * t, LANE), F32),
                        pltpu.VMEM((heads * t, HEAD), F32)],
        compiler_params=_params(("parallel", "arbitrary")),
        name="fox_sample_attn",
    )(fq, flat(cache_k), flat(cache_v), fk_new, fv_new, neg_cum_cache, neg_cum_new)


def _out_proj_kernel(x_ref, oa_ref, ob_ref, ga_ref, gb_ref, w_ref, o_ref):
    mix = (ga_ref[...].astype(F32) * oa_ref[...].astype(F32)
           + gb_ref[...].astype(F32) * ob_ref[...].astype(F32)).astype(BF)
    o_ref[...] = x_ref[...] + jnp.dot(mix, w_ref[...], preferred_element_type=F32)


def _out_proj(x, o_a, o_b, gates, w_o, tm=256):
    m, d = x.shape
    tm = _tile(m, tm)
    row = lambda i: (i, 0)
    return pl.pallas_call(
        _out_proj_kernel,
        grid=(m // tm,),
        in_specs=[pl.BlockSpec((tm, d), row), pl.BlockSpec((tm, d), row),
                  pl.BlockSpec((tm, d), row),
                  pl.BlockSpec((tm, d), lambda i: (i, 0)),
                  pl.BlockSpec((tm, d), lambda i: (i, 1)),
                  pl.BlockSpec((d, d), lambda i: (0, 0))],
        out_specs=pl.BlockSpec((tm, d), row),
        out_shape=jax.ShapeDtypeStruct((m, d), F32),
        compiler_params=_params(("parallel",)),
        name="gated_out_proj",
    )(x, o_a, o_b, gates, gates, w_o)


def _new_weight_tile(be_ref):
    i = pl.program_id(1)
    return (i == 0) | (be_ref[i] != be_ref[jnp.maximum(i - 1, 0)])


def _moe_up_kernel(be_ref, x_ref, wg_ref, wu_ref, bg_ref, bu_ref, o_ref, wg_sc, wu_sc):
    @pl.when(_new_weight_tile(be_ref))
    def _():
        wg_sc[...] = wg_ref[...].astype(BF)
        wu_sc[...] = wu_ref[...].astype(BF)

    x = x_ref[...]
    g = jnp.dot(x, wg_sc[...], preferred_element_type=F32) + bg_ref[...]
    u = jnp.dot(x, wu_sc[...], preferred_element_type=F32) + bu_ref[...]
    g = jnp.minimum(g, SWIGLU_LIMIT)
    u = jnp.clip(u, -SWIGLU_LIMIT, SWIGLU_LIMIT)
    o_ref[...] = (g * _sigmoid(SWIGLU_ALPHA * g) * (u + 1.0)).astype(o_ref.dtype)


def _moe_down_kernel(be_ref, a_ref, w_ref, b_ref, o_ref, w_sc):
    @pl.when(_new_weight_tile(be_ref))
    def _():
        w_sc[...] = w_ref[...].astype(BF)

    o_ref[...] = jnp.dot(a_ref[...], w_sc[...], preferred_element_type=F32) + b_ref[...]


def _moe_experts(xs, block_exp, w_gu, b_gu, w_down, b_down, bm, tf=512, tn=512):
    p, d = xs.shape
    e, _, ff2 = w_gu.shape
    ff = ff2 // 2
    tf = _tile(ff, tf)
    tn = _tile(d, tn)
    nf = ff // tf
    nb = p // bm
    act = pl.pallas_call(
        _moe_up_kernel,
        grid_spec=pltpu.PrefetchScalarGridSpec(
            num_scalar_prefetch=1,
            grid=(nf, nb),
            in_specs=[pl.BlockSpec((bm, d), lambda f, i, be: (i, 0)),
                      pl.BlockSpec((None, d, tf), lambda f, i, be: (be[i], 0, f)),
                      pl.BlockSpec((None, d, tf), lambda f, i, be: (be[i], 0, nf + f)),
                      pl.BlockSpec((None, 1, tf), lambda f, i, be: (be[i], 0, f)),
                      pl.BlockSpec((None, 1, tf), lambda f, i, be: (be[i], 0, nf + f))],
            out_specs=pl.BlockSpec((bm, tf), lambda f, i, be: (i, f)),
            scratch_shapes=[pltpu.VMEM((d, tf), BF), pltpu.VMEM((d, tf), BF)],
        ),
        out_shape=jax.ShapeDtypeStruct((p, ff), BF),
        compiler_params=_params(("parallel", "arbitrary")),
        name="moe_gate_up",
    )(block_exp, xs, w_gu, w_gu, b_gu.reshape(e, 1, ff2), b_gu.reshape(e, 1, ff2))
    return pl.pallas_call(
        _moe_down_kernel,
        grid_spec=pltpu.PrefetchScalarGridSpec(
            num_scalar_prefetch=1,
            grid=(d // tn, nb),
            in_specs=[pl.BlockSpec((bm, ff), lambda n, i, be: (i, 0)),
                      pl.BlockSpec((None, ff, tn), lambda n, i, be: (be[i], 0, n)),
                      pl.BlockSpec((None, 1, tn), lambda n, i, be: (be[i], 0, n))],
            out_specs=pl.BlockSpec((bm, tn), lambda n, i, be: (i, n)),
            scratch_shapes=[pltpu.VMEM((ff, tn), BF)],
        ),
        out_shape=jax.ShapeDtypeStruct((p, d), F32),
        compiler_params=_params(("parallel", "arbitrary")),
        name="moe_down",
    )(block_exp, act, w_down, b_down.reshape(e, 1, d))


def _moe(xn, logits, w_gu, b_gu, w_down, b_down, bm):
    n, d = xn.shape
    e = w_gu.shape[0]
    top_v, top_i = lax.top_k(logits, TOP_K)
    gates = jax.nn.softmax(top_v, axis=-1)
    nk = n * TOP_K
    a_exp = top_i.reshape(-1).astype(jnp.int32)
    onehot = (a_exp[:, None] == jnp.arange(e, dtype=jnp.int32)[None, :]).astype(jnp.int32)
    rank = jnp.sum((jnp.cumsum(onehot, axis=0) - onehot) * onehot, axis=1)
    counts = jnp.sum(onehot, axis=0)
    padded = (counts + bm - 1) // bm * bm
    pends = jnp.cumsum(padded)
    pstarts = pends - padded
    dest = pstarts[a_exp] + rank
    nb = -(-(nk + e * (bm - 1)) // bm)
    p = nb * bm
    a_tok = jnp.arange(nk, dtype=jnp.int32) // TOP_K
    buf_tok = jnp.full((p,), n, jnp.int32).at[dest].set(a_tok)
    block_exp = jnp.clip(jnp.searchsorted(pends, jnp.arange(nb, dtype=jnp.int32) * bm, side='right'),
                         0, e - 1).astype(jnp.int32)
    xpad = jnp.concatenate([xn, jnp.zeros((1, d), xn.dtype)], axis=0)
    xs = jnp.take(xpad, buf_tok, axis=0)
    out = _moe_experts(xs, block_exp, w_gu, b_gu, w_down, b_down, bm)
    dest = dest.reshape(n, TOP_K)
    return [jnp.take(out, dest[:, k], axis=0) for k in range(TOP_K)], gates


def _ple_kernel(x_ref, y0_ref, y1_ref, y2_ref, y3_ref, rw_ref, p_ref, wg_ref, bg_ref, wp_ref,
                gf_ref, o_ref, *, final):
    rw = rw_ref[...]
    x = x_ref[...] + ((rw[:, 0:1] * y0_ref[...] + rw[:, 1:2] * y1_ref[...])
                      + (rw[:, 2:3] * y2_ref[...] + rw[:, 3:4] * y3_ref[...]))
    gate = _sigmoid(jnp.dot(x.astype(BF), wg_ref[...], preferred_element_type=F32) + bg_ref[...])
    x = x + gate * jnp.dot(p_ref[...].astype(BF), wp_ref[...], preferred_element_type=F32)
    if final:
        x = _rms(x, gf_ref[...])
    o_ref[...] = x


def _ple(x, ys, route_w, row0, p_emb, w_plg, b_plg, w_pl, g_final, final, tm=256):
    m, d = x.shape
    pd = p_emb.shape[1]
    tm = _tile(m, tm)
    r0 = row0 // tm
    row = lambda i: (i, 0)
    yrow = lambda i: (i + r0, 0)
    fixed = lambda i: (0, 0)
    return pl.pallas_call(
        functools.partial(_ple_kernel, final=final),
        grid=(m // tm,),
        in_specs=[pl.BlockSpec((tm, d), row)] + [pl.BlockSpec((tm, d), yrow)] * TOP_K
                 + [pl.BlockSpec((tm, TOP_K), yrow), pl.BlockSpec((tm, pd), row),
                    pl.BlockSpec((d, d), fixed), pl.BlockSpec((1, d), fixed),
                    pl.BlockSpec((pd, d), fixed), pl.BlockSpec((1, d), fixed)],
        out_specs=pl.BlockSpec((tm, d), row),
        out_shape=jax.ShapeDtypeStruct((m, d), F32),
        compiler_params=_params(("parallel",)),
        name="ple_gate_final",
    )(x, *ys, route_w, p_emb, w_plg, b_plg.reshape(1, d), w_pl, g_final.reshape(1, d))


def _rot_cols(w):
    half = w.shape[-1] // 2
    return jnp.concatenate([-w[..., half:], w[..., :half]], axis=-1)


def _rope_table(pos):
    half = ROPE // 2
    freqs = ROPE_THETA ** (-jnp.arange(half, dtype=F32) / half)
    ang = pos.astype(F32)[:, None] * freqs[None, :]
    cos, sin = jnp.cos(ang), jnp.sin(ang)
    return jnp.concatenate([cos, cos, sin, sin], axis=-1)


def _layer_weights(w_in, b_forget, b_gate, w_q_up, w_uk, w_uv, heads, ql, kvl):
    d = w_in.shape[0]
    o = 0
    parts = []
    for s in (ql, kvl, ROPE, d, d, d, heads, 2 * d):
        parts.append(w_in[:, o:o + s])
        o += s
    w_cq, w_ckv, w_kr, w_fq, w_fk, w_fv, w_fg, w_gate = parts
    w_lat = jnp.concatenate(
        [w_cq, w_ckv, w_kr, _rot_cols(w_kr), jnp.pad(w_fg, ((0, 0), (0, LANE - heads)))], axis=1)
    wq_nope = w_q_up[:, :, :HEAD]
    wq_rope = w_q_up[:, :, HEAD:]
    wq = jnp.concatenate([wq_nope, wq_rope, _rot_cols(wq_rope)], axis=-1).reshape(ql, heads * QK_PAD)
    return dict(
        w_lat=w_lat.astype(BF), w_fq=w_fq.astype(BF), w_fk=w_fk.astype(BF), w_fv=w_fv.astype(BF),
        w_gate=w_gate.astype(BF), wq=wq.astype(BF),
        w_uk=w_uk.reshape(kvl, heads * HEAD).astype(BF),
        w_uv=w_uv.reshape(kvl, heads * HEAD).astype(BF),
        w_uk_t=jnp.transpose(w_uk, (1, 2, 0)).astype(BF),
        w_uv_h=jnp.transpose(w_uv, (1, 0, 2)).astype(BF),
        b_forget=jnp.pad(b_forget, (0, LANE - heads)).reshape(1, LANE),
        b_gate=b_gate.reshape(1, 2 * d))


def _trunc_bf16(x):
    bits = lax.bitcast_convert_type(x, jnp.uint32) & jnp.uint32(0xFFFF0000)
    return lax.bitcast_convert_type(bits, F32)


def _forget_bias_columns(logf, b, t, heads):
    cum = jnp.cumsum(logf.reshape(b, t, heads), axis=1)
    nck = -LOG2E * cum
    hi = _trunc_bf16(nck)
    mid = _trunc_bf16(nck - hi)
    lo = nck - hi - mid
    ext = jnp.stack([hi, mid, lo], axis=-1).astype(BF)
    ext = jnp.pad(ext, ((0, 0), (0, 0), (0, 0), (0, HEAD - 3)))
    return ext.reshape(b * t, heads * HEAD)


def _mixer_inputs(x, cs, lw, g_mix, g_q_lat, g_kv_lat, heads, ql, kvl, tm, prompt_bt=None):
    n, d = x.shape
    prompt = prompt_bt is not None
    h = _rmsnorm(x, g_mix, BF)
    tm = _tile(n, tm)
    row = lambda i, j: (i, 0)
    fixed = lambda i, j: (0, 0)
    wl = ql + kvl + 2 * LANE
    cqn, clat, clatb, kr, krb, logf = _mm(
        h, lw['w_lat'], functools.partial(_latent_epilogue, ql=ql, kvl=kvl),
        [jax.ShapeDtypeStruct((n, ql), BF), jax.ShapeDtypeStruct((n, kvl), F32),
         jax.ShapeDtypeStruct((n, kvl), BF), jax.ShapeDtypeStruct((n, LANE), F32),
         jax.ShapeDtypeStruct((n, LANE), BF), jax.ShapeDtypeStruct((n, LANE), F32)],
        [ql, kvl, kvl, LANE, LANE, LANE], tm=tm, tn=wl,
        extra=(g_q_lat.reshape(1, ql), g_kv_lat.reshape(1, kvl), lw['b_forget'], cs),
        extra_specs=(pl.BlockSpec((1, ql), fixed), pl.BlockSpec((1, kvl), fixed),
                     pl.BlockSpec((1, LANE), fixed), pl.BlockSpec((tm, LANE), row)),
        name="latent_proj")
    tn = _tile(d, 512)
    hpt = tn // HEAD
    if prompt:
        (fq,) = _mm(h, lw['w_fq'],
                    functools.partial(_fox_q_aug_epilogue, heads_per_tile=hpt,
                                      scale=FOX_SCALE * LOG2E),
                    [jax.ShapeDtypeStruct((n, heads * QK_PAD), BF)], [hpt * QK_PAD],
                    tm=tm, tn=tn, name="fox_q_proj")
        ext = _forget_bias_columns(logf[:, :heads], *prompt_bt, heads)
        fk, fkb = _mm(h, lw['w_fk'], functools.partial(_fox_k_aug_epilogue, heads_per_tile=hpt),
                      [jax.ShapeDtypeStruct((n, d), F32),
                       jax.ShapeDtypeStruct((n, heads * QK_PAD), BF)],
                      [tn, hpt * QK_PAD], tm=tm, tn=tn, extra=(ext,),
                      extra_specs=(pl.BlockSpec((tm, tn), lambda i, j: (i, j)),),
                      name="fox_k_proj")
    else:
        (fq,) = _mm(h, lw['w_fq'], functools.partial(_scaled_bf16_epilogue, scale=FOX_SCALE),
                    [jax.ShapeDtypeStruct((n, d), BF)], [tn], tm=tm, tn=tn, name="fox_q_proj")
        fk, fkb = _mm(h, lw['w_fk'], _dual_epilogue,
                      [jax.ShapeDtypeStruct((n, d), F32), jax.ShapeDtypeStruct((n, d), BF)],
                      [tn, tn], tm=tm, tn=tn, name="fox_k_proj")
    fv, fvb = _mm(h, lw['w_fv'], _dual_epilogue,
                  [jax.ShapeDtypeStruct((n, d), F32), jax.ShapeDtypeStruct((n, d), BF)],
                  [tn, tn], tm=tm, tn=tn, name="fox_v_proj")
    (gates,) = _mm(h, lw['w_gate'], _gate_epilogue, [jax.ShapeDtypeStruct((n, 2 * d), BF)], [tn],
                   tm=tm, tn=tn, extra=(lw['b_gate'],),
                   extra_specs=(pl.BlockSpec((1, tn), lambda i, j: (0, j)),), name="merge_gates")
    qh = 2 if heads % 2 == 0 else 1
    (q,) = _mm(cqn, lw['wq'],
               functools.partial(_q_epilogue, heads_per_tile=qh,
                                 scale=MLA_SCALE * LOG2E if prompt else MLA_SCALE),
               [jax.ShapeDtypeStruct((n, heads * QK_PAD), BF)], [qh * QK_PAD],
               tm=tm, tn=qh * QK_PAD, extra=(cs,),
               extra_specs=(pl.BlockSpec((tm, LANE), row),), name="mla_q_up")
    return dict(q=q, clat=clat, clatb=clatb, kr=kr, krb=krb, logf=logf, fq=fq, fk=fk, fkb=fkb,
                fv=fv, fvb=fvb, gates=gates)


def _prompt_attention(mi, lw, b, t, heads, kvl, tm):
    n = b * t
    d = heads * HEAD
    hpt = 2 if heads % 2 == 0 else 1
    tm = _tile(n, tm)
    (k,) = _mm(mi['clatb'], lw['w_uk'], functools.partial(_k_epilogue, heads_per_tile=hpt),
               [jax.ShapeDtypeStruct((n, heads * QK_PAD), BF)], [hpt * QK_PAD],
               tm=tm, tn=hpt * HEAD, extra=(mi['krb'],),
               extra_specs=(pl.BlockSpec((tm, LANE), lambda i, j: (i, 0)),), name="mla_k_up")
    (v,) = _mm(mi['clatb'], lw['w_uv'], _bf16_epilogue, [jax.ShapeDtypeStruct((n, d), BF)],
               [_tile(d, 512)], tm=tm, tn=_tile(d, 512), name="mla_v_up")
    o_a = _flash_prompt(mi['q'].reshape(b, t, -1), k.reshape(b, t, -1), v.reshape(b, t, -1),
                        heads, causal=False)
    o_b = _flash_prompt(mi['fq'].reshape(b, t, -1), mi['fkb'].reshape(b, t, -1),
                        mi['fvb'].reshape(b, t, -1), heads, causal=True)
    return o_a.reshape(n, d), o_b.reshape(n, d)


def _sample_attention(mi, lw, caches, layer, b, t, heads, kvl, tm):
    cache_lat, cache_rope, cache_k, cache_v, cache_logf = caches
    n = b * t
    d = heads * HEAD
    p = cache_lat.shape[2]
    tm = _tile(n, tm)
    (q_lat,) = _mm(mi['q'], lw['w_uk_t'], _bf16_epilogue,
                   [jax.ShapeDtypeStruct((n, heads * kvl), BF)], [kvl], tm=tm, tn=kvl,
                   a_spec=pl.BlockSpec((tm, HEAD), lambda i, j: (i, 2 * j)),
                   w_spec=pl.BlockSpec((None, HEAD, kvl), lambda i, j: (j, 0, 0)),
                   n_col_tiles=heads, name="mla_q_absorb")
    q_rope = mi['q'].reshape(n, heads, QK_PAD)[:, :, HEAD:]
    cache_rope_p = jnp.pad(cache_rope[layer], ((0, 0), (0, 0), (0, LANE - ROPE)))
    o_lat = _mla_sample(q_lat.reshape(b, t * heads, kvl), q_rope.reshape(b, t * heads, LANE),
                        cache_lat[layer], cache_rope_p, mi['clatb'].reshape(b, t, kvl),
                        mi['krb'].reshape(b, t, LANE))
    (o_a,) = _mm(o_lat.reshape(n, heads * kvl), lw['w_uv_h'], _bf16_epilogue,
                 [jax.ShapeDtypeStruct((n, d), BF)], [HEAD], tm=tm, tn=HEAD,
                 a_spec=pl.BlockSpec((tm, kvl), lambda i, j: (i, j)),
                 w_spec=pl.BlockSpec((None, kvl, HEAD), lambda i, j: (j, 0, 0)),
                 n_col_tiles=heads, name="mla_v_absorb")
    logf = mi['logf'][:, :heads].reshape(b, t, heads)
    cum = jnp.cumsum(jnp.concatenate([cache_logf[layer].astype(F32), logf], axis=1), axis=1)
    neg_cum = -cum.transpose(0, 2, 1)
    o_b = _fox_sample(mi['fq'].reshape(b, t, d), mi['fkb'].reshape(b, t, d),
                      mi['fvb'].reshape(b, t, d), cache_k, cache_v, layer,
                      neg_cum[:, :, None, :p], neg_cum[:, :, None, p:], heads)
    return o_a, o_b.reshape(n, d)


def kernel(x_prompt, x_sample, p_prompt, p_sample, cache_mla_latent, cache_mla_rope, cache_fox_k,
           cache_fox_v, cache_fox_logf, g_mix, w_in, b_forget, b_gate, g_q_lat, w_q_up, g_kv_lat,
           w_uk, w_uv, w_o, g_ffn, w_router, b_router, w_gu, b_gu, w_down, b_down, w_pl, w_plg,
           b_plg, g_final):
    bp, tp, d = x_prompt.shape
    bs, ts, _ = x_sample.shape
    depth = g_mix.shape[0]
    heads = d // HEAD
    ql = w_q_up.shape[1]
    kvl = w_uk.shape[1]
    past = cache_mla_latent.shape[2]
    n_p, n_s = bp * tp, bs * ts
    tm = 1024
    bm = 256

    cs_p = jnp.tile(_rope_table(jnp.arange(tp)), (bp, 1))
    cs_s = jnp.tile(_rope_table(past + jnp.arange(ts)), (bs, 1))
    x_p = x_prompt.reshape(n_p, d)
    x_s = x_sample.reshape(n_s, d)
    caches = (cache_mla_latent, cache_mla_rope, cache_fox_k, cache_fox_v, cache_fox_logf)
    st_p, st_s = [], []
    for i in range(depth):
        last = i == depth - 1
        lw = _layer_weights(w_in[i], b_forget[i], b_gate[i], w_q_up[i], w_uk[i], w_uv[i],
                            heads, ql, kvl)
        w_o_b = w_o[i].astype(BF)
        mi_p = _mixer_inputs(x_p, cs_p, lw, g_mix[i], g_q_lat[i], g_kv_lat[i], heads, ql, kvl, tm,
                             prompt_bt=(bp, tp))
        mi_s = _mixer_inputs(x_s, cs_s, lw, g_mix[i], g_q_lat[i], g_kv_lat[i], heads, ql, kvl, tm)
        oa_p, ob_p = _prompt_attention(mi_p, lw, bp, tp, heads, kvl, tm)
        oa_s, ob_s = _sample_attention(mi_s, lw, caches, i, bs, ts, heads, kvl, tm)
        x_p = _out_proj(x_p, oa_p, ob_p, mi_p['gates'], w_o_b)
        x_s = _out_proj(x_s, oa_s, ob_s, mi_s['gates'], w_o_b)
        xn_p, lg_p = _ffn_norm(x_p, g_ffn[i], w_router[i], b_router[i])
        xn_s, lg_s = _ffn_norm(x_s, g_ffn[i], w_router[i], b_router[i])
        ys, route_w = _moe(jnp.concatenate([xn_p, xn_s], axis=0),
                           jnp.concatenate([lg_p, lg_s], axis=0),
                           w_gu[i], b_gu[i], w_down[i], b_down[i], bm)
        w_plg_b = w_plg[i].astype(BF)
        w_pl_b = w_pl[i].astype(BF)
        x_p = _ple(x_p, ys, route_w, 0, p_prompt[i].reshape(n_p, -1), w_plg_b, b_plg[i], w_pl_b,
                   g_final, last)
        x_s = _ple(x_s, ys, route_w, n_p, p_sample[i].reshape(n_s, -1), w_plg_b, b_plg[i], w_pl_b,
                   g_final, last)
        for st, mi, b, t in ((st_p, mi_p, bp, tp), (st_s, mi_s, bs, ts)):
            st.append((mi['clat'].reshape(b, t, kvl),
                       mi['kr'][:, :ROPE].reshape(b, t, ROPE),
                       mi['fk'].reshape(b, t, heads, HEAD),
                       mi['fv'].reshape(b, t, heads, HEAD),
                       mi['logf'][:, :heads].reshape(b, t, heads)))
    outs = [x_p.reshape(bp, tp, d), x_s.reshape(bs, ts, d)]
    for st in (st_p, st_s):
        for k in range(5):
            outs.append(jnp.stack([s[k] for s in st]))
    return tuple(outs)
```

```python
import functools

import jax
import jax.numpy as jnp
from jax import lax
from jax.experimental import pallas as pl
from jax.experimental.pallas import tpu as pltpu

CHUNK = 64
NORM_EPS = 1e-6
HEAD = 128
ROPE = 64
QK_PAD = 256
MLA_SCALE = (HEAD + ROPE) ** -0.5
FOX_SCALE = HEAD ** -0.5
ROPE_THETA = 10000.0
TOP_K = 4
SWIGLU_LIMIT = 7.0
SWIGLU_ALPHA = 1.702
LANE = 128
VMEM_LIMIT = 56 * 1024 * 1024
NEG = -1e30
LOG2E = 1.4426950408889634
FLASH_BLOCK = 1024
FLASH_SUBBLOCKS = 2

BF = jnp.bfloat16
F32 = jnp.float32


def _tile(n, pref):
    if n <= pref:
        return n
    for t in range(pref, 7, -1):
        if n % t == 0 and t % 8 == 0:
            return t
    return n


def _params(sem):
    return pltpu.CompilerParams(dimension_semantics=sem, vmem_limit_bytes=VMEM_LIMIT)


def _rms(x, g):
    return x * lax.rsqrt(jnp.mean(x * x, axis=-1, keepdims=True) + NORM_EPS) * g


def _sigmoid(x):
    return 1.0 / (1.0 + jnp.exp(-x))


def _lanes(x, n):
    return jnp.tile(x, (1, n // LANE)) if n >= LANE else x[:, :n]


def _rmsnorm_kernel(x_ref, g_ref, o_ref):
    o_ref[...] = _rms(x_ref[...], g_ref[...]).astype(o_ref.dtype)


def _rmsnorm(x, g, out_dtype, tm=512):
    m, d = x.shape
    tm = _tile(m, tm)
    return pl.pallas_call(
        _rmsnorm_kernel,
        grid=(m // tm,),
        in_specs=[pl.BlockSpec((tm, d), lambda i: (i, 0)),
                  pl.BlockSpec((1, d), lambda i: (0, 0))],
        out_specs=pl.BlockSpec((tm, d), lambda i: (i, 0)),
        out_shape=jax.ShapeDtypeStruct((m, d), out_dtype),
        compiler_params=_params(("parallel",)),
        name="rmsnorm",
    )(x, g.reshape(1, d))


def _ffn_norm_kernel(x_ref, g_ref, wr_ref, br_ref, xn_ref, lg_ref):
    xn = _rms(x_ref[...], g_ref[...])
    xn_ref[...] = xn.astype(xn_ref.dtype)
    lg_ref[...] = jnp.dot(xn, wr_ref[...], preferred_element_type=F32,
                          precision=lax.Precision.HIGHEST) + br_ref[...]


def _ffn_norm(x, g, w_router, b_router, tm=512):
    m, d = x.shape
    e = w_router.shape[1]
    ep = -(-e // LANE) * LANE
    wr = jnp.pad(w_router, ((0, 0), (0, ep - e)))
    br = jnp.pad(b_router, (0, ep - e)).reshape(1, ep)
    tm = _tile(m, tm)
    xn, lg = pl.pallas_call(
        _ffn_norm_kernel,
        grid=(m // tm,),
        in_specs=[pl.BlockSpec((tm, d), lambda i: (i, 0)),
                  pl.BlockSpec((1, d), lambda i: (0, 0)),
                  pl.BlockSpec((d, ep), lambda i: (0, 0)),
                  pl.BlockSpec((1, ep), lambda i: (0, 0))],
        out_specs=[pl.BlockSpec((tm, d), lambda i: (i, 0)),
                   pl.BlockSpec((tm, ep), lambda i: (i, 0))],
        out_shape=[jax.ShapeDtypeStruct((m, d), BF),
                   jax.ShapeDtypeStruct((m, ep), F32)],
        compiler_params=_params(("parallel",)),
        name="ffn_norm_router",
    )(x, g.reshape(1, d), wr, br)
    return xn, lg[:, :e]


def _mm_kernel(*refs, n_extra, epilogue):
    a_ref, w_ref = refs[0], refs[1]
    extra = refs[2:2 + n_extra]
    outs = refs[2 + n_extra:]
    acc = jnp.dot(a_ref[...], w_ref[...], preferred_element_type=F32)
    epilogue(acc, extra, outs)


def _mm(a, w, epilogue, out_shapes, out_widths, *, tm, tn, extra=(), extra_specs=(),
        a_spec=None, w_spec=None, n_col_tiles=None, name="mm"):
    m = a.shape[0]
    k = a.shape[-1] if a_spec is None else None
    tm = _tile(m, tm)
    if n_col_tiles is None:
        n_col_tiles = w.shape[-1] // tn
    if a_spec is None:
        a_spec = pl.BlockSpec((tm, k), lambda i, j: (i, 0))
    if w_spec is None:
        w_spec = pl.BlockSpec((w.shape[0], tn), lambda i, j: (0, j))
    out_specs = [pl.BlockSpec((tm, ow), lambda i, j: (i, j)) for ow in out_widths]
    return pl.pallas_call(
        functools.partial(_mm_kernel, n_extra=len(extra), epilogue=epilogue),
        grid=(m // tm, n_col_tiles),
        in_specs=[a_spec, w_spec, *extra_specs],
        out_specs=out_specs,
        out_shape=out_shapes,
        compiler_params=_params(("parallel", "arbitrary")),
        name=name,
    )(a, w, *extra)


def _rope_mix(v, cs_ref):
    cs = cs_ref[...]
    lane = lax.broadcasted_iota(jnp.int32, cs.shape, 1)
    c = jnp.where(lane < ROPE, cs, 0.0)
    s = jnp.where(lane < ROPE, pltpu.roll(cs, ROPE, 1), 0.0)
    return v * c + pltpu.roll(v, ROPE, 1) * s


def _latent_epilogue(acc, extra, outs, *, ql, kvl):
    gq_ref, gkv_ref, bf_ref, cs_ref = extra
    cqn_ref, clat_ref, clatb_ref, kr_ref, krb_ref, logf_ref = outs
    cqn_ref[...] = _rms(acc[:, :ql], gq_ref[...]).astype(cqn_ref.dtype)
    clat = _rms(acc[:, ql:ql + kvl], gkv_ref[...])
    clat_ref[...] = clat
    clatb_ref[...] = clat.astype(clatb_ref.dtype)
    o = ql + kvl
    kr = _rope_mix(acc[:, o:o + LANE], cs_ref)
    kr_ref[...] = kr
    krb_ref[...] = kr.astype(krb_ref.dtype)
    fg = acc[:, o + LANE:o + 2 * LANE] + bf_ref[...]
    logf_ref[...] = jnp.minimum(fg, 0.0) - jnp.log(1.0 + jnp.exp(-jnp.abs(fg)))


def _scaled_bf16_epilogue(acc, extra, outs, *, scale):
    outs[0][...] = (acc * scale).astype(outs[0].dtype)


def _dual_epilogue(acc, extra, outs):
    outs[0][...] = acc
    outs[1][...] = acc.astype(outs[1].dtype)


def _gate_epilogue(acc, extra, outs):
    outs[0][...] = _sigmoid(acc + extra[0][...]).astype(outs[0].dtype)


def _q_epilogue(acc, extra, outs, *, heads_per_tile, scale):
    cs_ref = extra[0]
    o_ref = outs[0]
    for h in range(heads_per_tile):
        b = h * QK_PAD
        o_ref[:, b:b + HEAD] = (acc[:, b:b + HEAD] * scale).astype(o_ref.dtype)
        o_ref[:, b + HEAD:b + QK_PAD] = (
            _rope_mix(acc[:, b + HEAD:b + QK_PAD], cs_ref) * scale).astype(o_ref.dtype)


def _fox_q_aug_epilogue(acc, extra, outs, *, heads_per_tile, scale):
    o_ref = outs[0]
    lane = lax.broadcasted_iota(jnp.int32, (acc.shape[0], HEAD), 1)
    ones = jnp.where(lane < 3, 1.0, 0.0).astype(o_ref.dtype)
    for h in range(heads_per_tile):
        o_ref[:, h * QK_PAD:h * QK_PAD + HEAD] = (
            acc[:, h * HEAD:(h + 1) * HEAD] * scale).astype(o_ref.dtype)
        o_ref[:, h * QK_PAD + HEAD:(h + 1) * QK_PAD] = ones


def _fox_k_aug_epilogue(acc, extra, outs, *, heads_per_tile):
    ext_ref = extra[0]
    k_ref, o_ref = outs
    k_ref[...] = acc
    for h in range(heads_per_tile):
        o_ref[:, h * QK_PAD:h * QK_PAD + HEAD] = acc[:, h * HEAD:(h + 1) * HEAD].astype(o_ref.dtype)
        o_ref[:, h * QK_PAD + HEAD:(h + 1) * QK_PAD] = ext_ref[:, h * HEAD:(h + 1) * HEAD]


def _k_epilogue(acc, extra, outs, *, heads_per_tile):
    krb_ref = extra[0]
    o_ref = outs[0]
    for h in range(heads_per_tile):
        o_ref[:, h * QK_PAD:h * QK_PAD + HEAD] = acc[:, h * HEAD:(h + 1) * HEAD].astype(o_ref.dtype)
        o_ref[:, h * QK_PAD + HEAD:(h + 1) * QK_PAD] = krb_ref[...]


def _bf16_epilogue(acc, extra, outs):
    outs[0][...] = acc.astype(outs[0].dtype)


def _flash_prompt_kernel(q_ref, k_ref, v_ref, o_ref, m_sc, l_sc, acc_sc, *, t, nsub, causal):
    i = pl.program_id(2)
    m_sc[...] = jnp.full(m_sc.shape, NEG, F32)
    l_sc[...] = jnp.zeros(l_sc.shape, F32)
    acc_sc[...] = jnp.zeros(acc_sc.shape, F32)
    rep = t // LANE

    def load_kv(j):
        start = pl.multiple_of(j * t, t)
        return k_ref[pl.ds(start, t), :], v_ref[pl.ds(start, t), :]

    def update(u, k, v, masked):
        rows = slice(u * t, (u + 1) * t)
        s = lax.dot_general(q_ref[rows, :], k, (((1,), (1,)), ((), ())),
                            preferred_element_type=F32)
        if masked:
            qp = lax.broadcasted_iota(jnp.int32, s.shape, 0)
            kp = lax.broadcasted_iota(jnp.int32, s.shape, 1)
            keep = (kp <= qp) if causal else ((kp // CHUNK) <= (qp // CHUNK))
            s = jnp.where(keep, s, NEG)
        m_prev = m_sc[rows, :]
        m_new = jnp.maximum(m_prev, jnp.max(s, axis=-1, keepdims=True))
        alpha = jnp.exp2(m_prev - m_new)
        p = jnp.exp2(s - jnp.tile(m_new, (1, rep)))
        l_sc[rows, :] = alpha * l_sc[rows, :] + jnp.sum(p, axis=-1, keepdims=True)
        acc_sc[rows, :] = alpha * acc_sc[rows, :] + jnp.dot(p.astype(v.dtype), v,
                                                            preferred_element_type=F32)
        m_sc[rows, :] = m_new

    def body(j, carry):
        k, v = load_kv(j)
        for u in range(nsub):
            update(u, k, v, False)
        return carry

    lax.fori_loop(0, nsub * i, body, 0)
    for w in range(nsub):
        k, v = load_kv(nsub * i + w)
        for u in range(w, nsub):
            update(u, k, v, u == w)
    o_ref[...] = (acc_sc[...] / l_sc[...]).astype(o_ref.dtype)


def _flash_prompt(q, k, v, heads, causal, t=None, nsub=None):
    b, tt, _ = q.shape
    t = _tile(tt, FLASH_BLOCK if t is None else t)
    nsub = FLASH_SUBBLOCKS if nsub is None else nsub
    if tt % (nsub * t):
        nsub = 1
    tq = nsub * t
    return pl.pallas_call(
        functools.partial(_flash_prompt_kernel, t=t, nsub=nsub, causal=causal),
        grid=(b, heads, tt // tq),
        in_specs=[pl.BlockSpec((None, tq, QK_PAD), lambda b_, h, i: (b_, i, h)),
                  pl.BlockSpec((None, tt, QK_PAD), lambda b_, h, i: (b_, 0, h)),
                  pl.BlockSpec((None, tt, HEAD), lambda b_, h, i: (b_, 0, h))],
        out_specs=pl.BlockSpec((None, tq, HEAD), lambda b_, h, i: (b_, i, h)),
        out_shape=jax.ShapeDtypeStruct((b, tt, heads * HEAD), BF),
        scratch_shapes=[pltpu.VMEM((tq, LANE), F32), pltpu.VMEM((tq, LANE), F32),
                        pltpu.VMEM((tq, HEAD), F32)],
        compiler_params=_params(("parallel", "parallel", "arbitrary")),
        name="fox_prompt_attn" if causal else "mla_prompt_attn",
    )(q, k, v)


def _mla_sample_kernel(ql_ref, qr_ref, cc_ref, cr_ref, cn_ref, rn_ref, o_ref, m_sc, l_sc, acc_sc):
    j = pl.program_id(1)

    @pl.when(j == 0)
    def _():
        m_sc[...] = jnp.full(m_sc.shape, NEG, F32)
        l_sc[...] = jnp.zeros(l_sc.shape, F32)
        acc_sc[...] = jnp.zeros(acc_sc.shape, F32)

    ql = ql_ref[...]
    qr = qr_ref[...]

    def update(c, r):
        s = (lax.dot_general(ql, c, (((1,), (1,)), ((), ())), preferred_element_type=F32)
             + lax.dot_general(qr, r, (((1,), (1,)), ((), ())), preferred_element_type=F32))
        m_prev = m_sc[...]
        m_new = jnp.maximum(m_prev, jnp.max(s, axis=-1, keepdims=True))
        alpha = jnp.exp(m_prev - m_new)
        p = jnp.exp(s - _lanes(m_new, s.shape[1]))
        l_sc[...] = alpha * l_sc[...] + jnp.sum(p, axis=-1, keepdims=True)
        acc_sc[...] = _lanes(alpha, c.shape[1]) * acc_sc[...] + jnp.dot(
            p.astype(c.dtype), c, preferred_element_type=F32)
        m_sc[...] = m_new

    update(cc_ref[...].astype(BF), cr_ref[...].astype(BF))

    @pl.when(j == pl.num_programs(1) - 1)
    def _():
        update(cn_ref[...], rn_ref[...])
        o_ref[...] = (acc_sc[...] / _lanes(l_sc[...], o_ref.shape[1])).astype(o_ref.dtype)


def _mla_sample(q_lat, q_rope, cache_lat, cache_rope, c_new, r_new, tk=512):
    b, r, c = q_lat.shape
    p = cache_lat.shape[1]
    tn = c_new.shape[1]
    tk = _tile(p, tk)
    return pl.pallas_call(
        _mla_sample_kernel,
        grid=(b, p // tk),
        in_specs=[pl.BlockSpec((None, r, c), lambda b_, j: (b_, 0, 0)),
                  pl.BlockSpec((None, r, LANE), lambda b_, j: (b_, 0, 0)),
                  pl.BlockSpec((None, tk, c), lambda b_, j: (b_, j, 0)),
                  pl.BlockSpec((None, tk, LANE), lambda b_, j: (b_, j, 0)),
                  pl.BlockSpec((None, tn, c), lambda b_, j: (b_, 0, 0)),
                  pl.BlockSpec((None, tn, LANE), lambda b_, j: (b_, 0, 0))],
        out_specs=pl.BlockSpec((None, r, c), lambda b_, j: (b_, 0, 0)),
        out_shape=jax.ShapeDtypeStruct((b, r, c), BF),
        scratch_shapes=[pltpu.VMEM((r, LANE), F32), pltpu.VMEM((r, LANE), F32),
                        pltpu.VMEM((r, c), F32)],
        compiler_params=_params(("parallel", "arbitrary")),
        name="mla_sample_attn",
    )(q_lat, q_rope, cache_lat, cache_rope, c_new, r_new)


def _fox_sample_kernel(q_ref, ck_ref, cv_ref, kn_ref, vn_ref, cc_ref, cn_ref,
                       o_ref, m_sc, l_sc, acc_sc, *, heads):
    j = pl.program_id(1)
    t = q_ref.shape[0]
    head_cols = lambda h: slice(h * HEAD, (h + 1) * HEAD)
    head_rows = lambda h: slice(h * t, (h + 1) * t)

    @pl.when(j == 0)
    def _():
        m_sc[...] = jnp.full(m_sc.shape, NEG, F32)
        l_sc[...] = jnp.zeros(l_sc.shape, F32)
        acc_sc[...] = jnp.zeros(acc_sc.shape, F32)

    def update(keys, vals, neg_cum, causal):
        s = jnp.concatenate(
            [lax.dot_general(q_ref[:, head_cols(h)], keys(h), (((1,), (1,)), ((), ())),
                             preferred_element_type=F32) + neg_cum(h) for h in range(heads)],
            axis=0)
        if causal:
            qp = lax.broadcasted_iota(jnp.int32, s.shape, 0) % t
            kp = lax.broadcasted_iota(jnp.int32, s.shape, 1)
            s = jnp.where(kp <= qp, s, NEG)
        m_prev = m_sc[...]
        m_new = jnp.maximum(m_prev, jnp.max(s, axis=-1, keepdims=True))
        alpha = jnp.exp(m_prev - m_new)
        p = jnp.exp(s - _lanes(m_new, s.shape[1]))
        l_sc[...] = alpha * l_sc[...] + jnp.sum(p, axis=-1, keepdims=True)
        p = p.astype(BF)
        pv = jnp.concatenate(
            [jnp.dot(p[head_rows(h)], vals(h), preferred_element_type=F32) for h in range(heads)],
            axis=0)
        acc_sc[...] = alpha * acc_sc[...] + pv
        m_sc[...] = m_new

    tk = ck_ref.shape[0] // heads
    cache_rows = lambda h: pl.ds(h, tk, stride=heads)
    update(lambda h: ck_ref[cache_rows(h), :].astype(BF),
           lambda h: cv_ref[cache_rows(h), :].astype(BF),
           lambda h: cc_ref[h], False)

    @pl.when(j == pl.num_programs(1) - 1)
    def _():
        update(lambda h: kn_ref[:, head_cols(h)], lambda h: vn_ref[:, head_cols(h)],
               lambda h: cn_ref[h], True)
        o = acc_sc[...] / l_sc[...]
        for h in range(heads):
            o_ref[:, head_cols(h)] = o[head_rows(h)].astype(o_ref.dtype)


def _fox_sample(fq, fk_new, fv_new, cache_k, cache_v, layer, neg_cum_cache, neg_cum_new, heads,
                tk=512):
    b, t, d = fq.shape
    p = cache_k.shape[2]
    tk = _tile(p, tk)
    cache_spec = pl.BlockSpec((None, None, tk * heads, HEAD), lambda b_, j: (layer, b_, j, 0))
    flat = lambda c: c.reshape(c.shape[0], b, p * heads, HEAD)
    return pl.pallas_call(
        functools.partial(_fox_sample_kernel, heads=heads),
        grid=(b, p // tk),
        in_specs=[pl.BlockSpec((None, t, d), lambda b_, j: (b_, 0, 0)),
                  cache_spec, cache_spec,
                  pl.BlockSpec((None, t, d), lambda b_, j: (b_, 0, 0)),
                  pl.BlockSpec((None, t, d), lambda b_, j: (b_, 0, 0)),
                  pl.BlockSpec((None, heads, 1, tk), lambda b_, j: (b_, 0, 0, j)),
                  pl.BlockSpec((None, heads, 1, t), lambda b_, j: (b_, 0, 0, 0))],
        out_specs=pl.BlockSpec((None, t, d), lambda b_, j: (b_, 0, 0)),
        out_shape=jax.ShapeDtypeStruct((b, t, d), BF),
        scratch_shapes=[pltpu.VMEM((heads * t, LANE), F32), pltpu.VMEM((heads * t, LANE), F32),
                        pltpu.VMEM((heads * t, HEAD), F32)],
        compiler_params=_params(("parallel", "arbitrary")),
        name="fox_sample_attn",
    )(fq, flat(cache_k), flat(cache_v), fk_new, fv_new, neg_cum_cache, neg_cum_new)


def _out_proj_kernel(x_ref, oa_ref, ob_ref, ga_ref, gb_ref, w_ref, o_ref):
    mix = (ga_ref[...].astype(F32) * oa_ref[...].astype(F32)
           + gb_ref[...].astype(F32) * ob_ref[...].astype(F32)).astype(BF)
    o_ref[...] = x_ref[...] + jnp.dot(mix, w_ref[...], preferred_element_type=F32)


def _out_proj(x, o_a, o_b, gates, w_o, tm=256):
    m, d = x.shape
    tm = _tile(m, tm)
    row = lambda i: (i, 0)
    return pl.pallas_call(
        _out_proj_kernel,
        grid=(m // tm,),
        in_specs=[pl.BlockSpec((tm, d), row), pl.BlockSpec((tm, d), row),
                  pl.BlockSpec((tm, d), row),
                  pl.BlockSpec((tm, d), lambda i: (i, 0)),
                  pl.BlockSpec((tm, d), lambda i: (i, 1)),
                  pl.BlockSpec((d, d), lambda i: (0, 0))],
        out_specs=pl.BlockSpec((tm, d), row),
        out_shape=jax.ShapeDtypeStruct((m, d), F32),
        compiler_params=_params(("parallel",)),
        name="gated_out_proj",
    )(x, o_a, o_b, gates, gates, w_o)


def _new_weight_tile(be_ref):
    i = pl.program_id(1)
    return (i == 0) | (be_ref[i] != be_ref[jnp.maximum(i - 1, 0)])


def _moe_up_kernel(be_ref, nu_ref, x_ref, wg_ref, wu_ref, bg_ref, bu_ref, o_ref, wg_sc, wu_sc):
    used = pl.program_id(1) < nu_ref[0]

    @pl.when(used & _new_weight_tile(be_ref))
    def _():
        wg_sc[...] = wg_ref[...].astype(BF)
        wu_sc[...] = wu_ref[...].astype(BF)

    @pl.when(used)
    def _():
        x = x_ref[...]
        g = jnp.dot(x, wg_sc[...], preferred_element_type=F32) + bg_ref[...]
        u = jnp.dot(x, wu_sc[...], preferred_element_type=F32) + bu_ref[...]
        g = jnp.minimum(g, SWIGLU_LIMIT)
        u = jnp.clip(u, -SWIGLU_LIMIT, SWIGLU_LIMIT)
        o_ref[...] = (g * _sigmoid(SWIGLU_ALPHA * g) * (u + 1.0)).astype(o_ref.dtype)


def _moe_down_kernel(be_ref, nu_ref, a_ref, w_ref, b_ref, o_ref, w_sc):
    used = pl.program_id(1) < nu_ref[0]

    @pl.when(used & _new_weight_tile(be_ref))
    def _():
        w_sc[...] = w_ref[...].astype(BF)

    @pl.when(used)
    def _():
        o_ref[...] = jnp.dot(a_ref[...], w_sc[...], preferred_element_type=F32) + b_ref[...]


def _moe_experts(xs, block_exp, n_used, w_gu, b_gu, w_down, b_down, bm, tf=512, tn=512):
    p, d = xs.shape
    e, _, ff2 = w_gu.shape
    ff = ff2 // 2
    tf = _tile(ff, tf)
    tn = _tile(d, tn)
    nf = ff // tf
    nb = p // bm
    blk = lambda i, nu: jnp.minimum(i, nu[0] - 1)
    exp = lambda i, be, nu: be[blk(i, nu)]
    act = pl.pallas_call(
        _moe_up_kernel,
        grid_spec=pltpu.PrefetchScalarGridSpec(
            num_scalar_prefetch=2,
            grid=(nf, nb),
            in_specs=[pl.BlockSpec((bm, d), lambda f, i, be, nu: (blk(i, nu), 0)),
                      pl.BlockSpec((None, d, tf), lambda f, i, be, nu: (exp(i, be, nu), 0, f)),
                      pl.BlockSpec((None, d, tf), lambda f, i, be, nu: (exp(i, be, nu), 0, nf + f)),
                      pl.BlockSpec((None, 1, tf), lambda f, i, be, nu: (exp(i, be, nu), 0, f)),
                      pl.BlockSpec((None, 1, tf), lambda f, i, be, nu: (exp(i, be, nu), 0, nf + f))],
            out_specs=pl.BlockSpec((bm, tf), lambda f, i, be, nu: (blk(i, nu), f)),
            scratch_shapes=[pltpu.VMEM((d, tf), BF), pltpu.VMEM((d, tf), BF)],
        ),
        out_shape=jax.ShapeDtypeStruct((p, ff), BF),
        compiler_params=_params(("parallel", "arbitrary")),
        name="moe_gate_up",
    )(block_exp, n_used, xs, w_gu, w_gu, b_gu.reshape(e, 1, ff2), b_gu.reshape(e, 1, ff2))
    return pl.pallas_call(
        _moe_down_kernel,
        grid_spec=pltpu.PrefetchScalarGridSpec(
            num_scalar_prefetch=2,
            grid=(d // tn, nb),
            in_specs=[pl.BlockSpec((bm, ff), lambda n, i, be, nu: (blk(i, nu), 0)),
                      pl.BlockSpec((None, ff, tn), lambda n, i, be, nu: (exp(i, be, nu), 0, n)),
                      pl.BlockSpec((None, 1, tn), lambda n, i, be, nu: (exp(i, be, nu), 0, n))],
            out_specs=pl.BlockSpec((bm, tn), lambda n, i, be, nu: (blk(i, nu), n)),
            scratch_shapes=[pltpu.VMEM((ff, tn), BF)],
        ),
        out_shape=jax.ShapeDtypeStruct((p, d), F32),
        compiler_params=_params(("parallel", "arbitrary")),
        name="moe_down",
    )(block_exp, n_used, act, w_down, b_down.reshape(e, 1, d))


def _rows(x, idx):
    return x.at[idx].get(mode='promise_in_bounds', unique_indices=False)


def _moe(xn, logits, w_gu, b_gu, w_down, b_down, bm):
    n, d = xn.shape
    e = w_gu.shape[0]
    top_v, top_i = lax.top_k(logits, TOP_K)
    gates = jax.nn.softmax(top_v, axis=-1)
    nk = n * TOP_K
    a_exp = top_i.reshape(-1).astype(jnp.int32)
    onehot = (a_exp[:, None] == jnp.arange(e, dtype=jnp.int32)[None, :]).astype(jnp.int32)
    rank = jnp.sum((jnp.cumsum(onehot, axis=0) - onehot) * onehot, axis=1)
    counts = jnp.sum(onehot, axis=0)
    padded = (counts + bm - 1) // bm * bm
    pends = jnp.cumsum(padded)
    pstarts = pends - padded
    dest = pstarts[a_exp] + rank
    nb = -(-(nk + e * (bm - 1)) // bm)
    p = nb * bm
    a_tok = jnp.arange(nk, dtype=jnp.int32) // TOP_K
    buf_tok = jnp.full((p,), n, jnp.int32).at[dest].set(a_tok)
    block_start = jnp.arange(nb, dtype=jnp.int32) * bm
    block_exp = jnp.minimum(jnp.sum((pends[None, :] <= block_start[:, None]).astype(jnp.int32), axis=1),
                            e - 1)
    n_used = (pends[-1:] // bm).astype(jnp.int32)
    xpad = jnp.concatenate([xn, jnp.zeros((1, d), xn.dtype)], axis=0)
    xs = _rows(xpad, buf_tok)
    out = _moe_experts(xs, block_exp, n_used, w_gu, b_gu, w_down, b_down, bm)
    dest = dest.reshape(n, TOP_K)
    return [_rows(out, dest[:, k]) for k in range(TOP_K)], gates


def _ple_kernel(x_ref, y0_ref, y1_ref, y2_ref, y3_ref, rw_ref, p_ref, wg_ref, bg_ref, wp_ref,
                gf_ref, o_ref, *, final):
    rw = rw_ref[...]
    x = x_ref[...] + ((rw[:, 0:1] * y0_ref[...] + rw[:, 1:2] * y1_ref[...])
                      + (rw[:, 2:3] * y2_ref[...] + rw[:, 3:4] * y3_ref[...]))
    gate = _sigmoid(jnp.dot(x.astype(BF), wg_ref[...], preferred_element_type=F32) + bg_ref[...])
    x = x + gate * jnp.dot(p_ref[...].astype(BF), wp_ref[...], preferred_element_type=F32)
    if final:
        x = _rms(x, gf_ref[...])
    o_ref[...] = x


def _ple(x, ys, route_w, row0, p_emb, w_plg, b_plg, w_pl, g_final, final, tm=256):
    m, d = x.shape
    pd = p_emb.shape[1]
    tm = _tile(m, tm)
    r0 = row0 // tm
    row = lambda i: (i, 0)
    yrow = lambda i: (i + r0, 0)
    fixed = lambda i: (0, 0)
    return pl.pallas_call(
        functools.partial(_ple_kernel, final=final),
        grid=(m // tm,),
        in_specs=[pl.BlockSpec((tm, d), row)] + [pl.BlockSpec((tm, d), yrow)] * TOP_K
                 + [pl.BlockSpec((tm, TOP_K), yrow), pl.BlockSpec((tm, pd), row),
                    pl.BlockSpec((d, d), fixed), pl.BlockSpec((1, d), fixed),
                    pl.BlockSpec((pd, d), fixed), pl.BlockSpec((1, d), fixed)],
        out_specs=pl.BlockSpec((tm, d), row),
        out_shape=jax.ShapeDtypeStruct((m, d), F32),
        compiler_params=_params(("parallel",)),
        name="ple_gate_final",
    )(x, *ys, route_w, p_emb, w_plg, b_plg.reshape(1, d), w_pl, g_final.reshape(1, d))


def _rot_cols(w):
    half = w.shape[-1] // 2
    return jnp.concatenate([-w[..., half:], w[..., :half]], axis=-1)


def _rope_table(pos):
    half = ROPE // 2
    freqs = ROPE_THETA ** (-jnp.arange(half, dtype=F32) / half)
    ang = pos.astype(F32)[:, None] * freqs[None, :]
    cos, sin = jnp.cos(ang), jnp.sin(ang)
    return jnp.concatenate([cos, cos, sin, sin], axis=-1)


def _layer_weights(w_in, b_forget, b_gate, w_q_up, w_uk, w_uv, heads, ql, kvl):
    d = w_in.shape[0]
    o = 0
    parts = []
    for s in (ql, kvl, ROPE, d, d, d, heads, 2 * d):
        parts.append(w_in[:, o:o + s])
        o += s
    w_cq, w_ckv, w_kr, w_fq, w_fk, w_fv, w_fg, w_gate = parts
    w_lat = jnp.concatenate(
        [w_cq, w_ckv, w_kr, _rot_cols(w_kr), jnp.pad(w_fg, ((0, 0), (0, LANE - heads)))], axis=1)
    wq_nope = w_q_up[:, :, :HEAD]
    wq_rope = w_q_up[:, :, HEAD:]
    wq = jnp.concatenate([wq_nope, wq_rope, _rot_cols(wq_rope)], axis=-1).reshape(ql, heads * QK_PAD)
    return dict(
        w_lat=w_lat.astype(BF), w_fq=w_fq.astype(BF), w_fk=w_fk.astype(BF), w_fv=w_fv.astype(BF),
        w_gate=w_gate.astype(BF), wq=wq.astype(BF),
        w_uk=w_uk.reshape(kvl, heads * HEAD).astype(BF),
        w_uv=w_uv.reshape(kvl, heads * HEAD).astype(BF),
        w_uk_t=jnp.transpose(w_uk, (1, 2, 0)).astype(BF),
        w_uv_h=jnp.transpose(w_uv, (1, 0, 2)).astype(BF),
        b_forget=jnp.pad(b_forget, (0, LANE - heads)).reshape(1, LANE),
        b_gate=b_gate.reshape(1, 2 * d))


def _trunc_bf16(x):
    bits = lax.bitcast_convert_type(x, jnp.uint32) & jnp.uint32(0xFFFF0000)
    return lax.bitcast_convert_type(bits, F32)


def _forget_bias_columns(logf, b, t, heads):
    cum = jnp.cumsum(logf.reshape(b, t, heads), axis=1)
    nck = -LOG2E * cum
    hi = _trunc_bf16(nck)
    mid = _trunc_bf16(nck - hi)
    lo = nck - hi - mid
    ext = jnp.stack([hi, mid, lo], axis=-1).astype(BF)
    ext = jnp.pad(ext, ((0, 0), (0, 0), (0, 0), (0, HEAD - 3)))
    return ext.reshape(b * t, heads * HEAD)


def _mixer_inputs(x, cs, lw, g_mix, g_q_lat, g_kv_lat, heads, ql, kvl, tm, prompt_bt=None):
    n, d = x.shape
    prompt = prompt_bt is not None
    h = _rmsnorm(x, g_mix, BF)
    tm = _tile(n, tm)
    row = lambda i, j: (i, 0)
    fixed = lambda i, j: (0, 0)
    wl = ql + kvl + 2 * LANE
    cqn, clat, clatb, kr, krb, logf = _mm(
        h, lw['w_lat'], functools.partial(_latent_epilogue, ql=ql, kvl=kvl),
        [jax.ShapeDtypeStruct((n, ql), BF), jax.ShapeDtypeStruct((n, kvl), F32),
         jax.ShapeDtypeStruct((n, kvl), BF), jax.ShapeDtypeStruct((n, LANE), F32),
         jax.ShapeDtypeStruct((n, LANE), BF), jax.ShapeDtypeStruct((n, LANE), F32)],
        [ql, kvl, kvl, LANE, LANE, LANE], tm=tm, tn=wl,
        extra=(g_q_lat.reshape(1, ql), g_kv_lat.reshape(1, kvl), lw['b_forget'], cs),
        extra_specs=(pl.BlockSpec((1, ql), fixed), pl.BlockSpec((1, kvl), fixed),
                     pl.BlockSpec((1, LANE), fixed), pl.BlockSpec((tm, LANE), row)),
        name="latent_proj")
    tn = _tile(d, 512)
    hpt = tn // HEAD
    if prompt:
        (fq,) = _mm(h, lw['w_fq'],
                    functools.partial(_fox_q_aug_epilogue, heads_per_tile=hpt,
                                      scale=FOX_SCALE * LOG2E),
                    [jax.ShapeDtypeStruct((n, heads * QK_PAD), BF)], [hpt * QK_PAD],
                    tm=tm, tn=tn, name="fox_q_proj")
        ext = _forget_bias_columns(logf[:, :heads], *prompt_bt, heads)
        fk, fkb = _mm(h, lw['w_fk'], functools.partial(_fox_k_aug_epilogue, heads_per_tile=hpt),
                      [jax.ShapeDtypeStruct((n, d), F32),
                       jax.ShapeDtypeStruct((n, heads * QK_PAD), BF)],
                      [tn, hpt * QK_PAD], tm=tm, tn=tn, extra=(ext,),
                      extra_specs=(pl.BlockSpec((tm, tn), lambda i, j: (i, j)),),
                      name="fox_k_proj")
    else:
        (fq,) = _mm(h, lw['w_fq'], functools.partial(_scaled_bf16_epilogue, scale=FOX_SCALE),
                    [jax.ShapeDtypeStruct((n, d), BF)], [tn], tm=tm, tn=tn, name="fox_q_proj")
        fk, fkb = _mm(h, lw['w_fk'], _dual_epilogue,
                      [jax.ShapeDtypeStruct((n, d), F32), jax.ShapeDtypeStruct((n, d), BF)],
                      [tn, tn], tm=tm, tn=tn, name="fox_k_proj")
    fv, fvb = _mm(h, lw['w_fv'], _dual_epilogue,
                  [jax.ShapeDtypeStruct((n, d), F32), jax.ShapeDtypeStruct((n, d), BF)],
                  [tn, tn], tm=tm, tn=tn, name="fox_v_proj")
    (gates,) = _mm(h, lw['w_gate'], _gate_epilogue, [jax.ShapeDtypeStruct((n, 2 * d), BF)], [tn],
                   tm=tm, tn=tn, extra=(lw['b_gate'],),
                   extra_specs=(pl.BlockSpec((1, tn), lambda i, j: (0, j)),), name="merge_gates")
    qh = 2 if heads % 2 == 0 else 1
    (q,) = _mm(cqn, lw['wq'],
               functools.partial(_q_epilogue, heads_per_tile=qh,
                                 scale=MLA_SCALE * LOG2E if prompt else MLA_SCALE),
               [jax.ShapeDtypeStruct((n, heads * QK_PAD), BF)], [qh * QK_PAD],
               tm=tm, tn=qh * QK_PAD, extra=(cs,),
               extra_specs=(pl.BlockSpec((tm, LANE), row),), name="mla_q_up")
    return dict(q=q, clat=clat, clatb=clatb, kr=kr, krb=krb, logf=logf, fq=fq, fk=fk, fkb=fkb,
                fv=fv, fvb=fvb, gates=gates)


def _prompt_attention(mi, lw, b, t, heads, kvl, tm):
    n = b * t
    d = heads * HEAD
    hpt = 2 if heads % 2 == 0 else 1
    tm = _tile(n, tm)
    (k,) = _mm(mi['clatb'], lw['w_uk'], functools.partial(_k_epilogue, heads_per_tile=hpt),
               [jax.ShapeDtypeStruct((n, heads * QK_PAD), BF)], [hpt * QK_PAD],
               tm=tm, tn=hpt * HEAD, extra=(mi['krb'],),
               extra_specs=(pl.BlockSpec((tm, LANE), lambda i, j: (i, 0)),), name="mla_k_up")
    (v,) = _mm(mi['clatb'], lw['w_uv'], _bf16_epilogue, [jax.ShapeDtypeStruct((n, d), BF)],
               [_tile(d, 512)], tm=tm, tn=_tile(d, 512), name="mla_v_up")
    o_a = _flash_prompt(mi['q'].reshape(b, t, -1), k.reshape(b, t, -1), v.reshape(b, t, -1),
                        heads, causal=False)
    o_b = _flash_prompt(mi['fq'].reshape(b, t, -1), mi['fkb'].reshape(b, t, -1),
                        mi['fvb'].reshape(b, t, -1), heads, causal=True)
    return o_a.reshape(n, d), o_b.reshape(n, d)


def _sample_attention(mi, lw, caches, layer, b, t, heads, kvl, tm):
    cache_lat, cache_rope, cache_k, cache_v, cache_logf = caches
    n = b * t
    d = heads * HEAD
    p = cache_lat.shape[2]
    tm = _tile(n, tm)
    (q_lat,) = _mm(mi['q'], lw['w_uk_t'], _bf16_epilogue,
                   [jax.ShapeDtypeStruct((n, heads * kvl), BF)], [kvl], tm=tm, tn=kvl,
                   a_spec=pl.BlockSpec((tm, HEAD), lambda i, j: (i, 2 * j)),
                   w_spec=pl.BlockSpec((None, HEAD, kvl), lambda i, j: (j, 0, 0)),
                   n_col_tiles=heads, name="mla_q_absorb")
    q_rope = mi['q'].reshape(n, heads, QK_PAD)[:, :, HEAD:]
    cache_rope_p = jnp.pad(cache_rope[layer], ((0, 0), (0, 0), (0, LANE - ROPE)))
    o_lat = _mla_sample(q_lat.reshape(b, t * heads, kvl), q_rope.reshape(b, t * heads, LANE),
                        cache_lat[layer], cache_rope_p, mi['clatb'].reshape(b, t, kvl),
                        mi['krb'].reshape(b, t, LANE))
    (o_a,) = _mm(o_lat.reshape(n, heads * kvl), lw['w_uv_h'], _bf16_epilogue,
                 [jax.ShapeDtypeStruct((n, d), BF)], [HEAD], tm=tm, tn=HEAD,
                 a_spec=pl.BlockSpec((tm, kvl), lambda i, j: (i, j)),
                 w_spec=pl.BlockSpec((None, kvl, HEAD), lambda i, j: (j, 0, 0)),
                 n_col_tiles=heads, name="mla_v_absorb")
    logf = mi['logf'][:, :heads].reshape(b, t, heads)
    cum = jnp.cumsum(jnp.concatenate([cache_logf[layer].astype(F32), logf], axis=1), axis=1)
    neg_cum = -cum.transpose(0, 2, 1)
    o_b = _fox_sample(mi['fq'].reshape(b, t, d), mi['fkb'].reshape(b, t, d),
                      mi['fvb'].reshape(b, t, d), cache_k, cache_v, layer,
                      neg_cum[:, :, None, :p], neg_cum[:, :, None, p:], heads)
    return o_a, o_b.reshape(n, d)


def kernel(x_prompt, x_sample, p_prompt, p_sample, cache_mla_latent, cache_mla_rope, cache_fox_k,
           cache_fox_v, cache_fox_logf, g_mix, w_in, b_forget, b_gate, g_q_lat, w_q_up, g_kv_lat,
           w_uk, w_uv, w_o, g_ffn, w_router, b_router, w_gu, b_gu, w_down, b_down, w_pl, w_plg,
           b_plg, g_final):
    bp, tp, d = x_prompt.shape
    bs, ts, _ = x_sample.shape
    depth = g_mix.shape[0]
    heads = d // HEAD
    ql = w_q_up.shape[1]
    kvl = w_uk.shape[1]
    past = cache_mla_latent.shape[2]
    n_p, n_s = bp * tp, bs * ts
    tm = 1024
    bm = 512

    cs_p = jnp.tile(_rope_table(jnp.arange(tp)), (bp, 1))
    cs_s = jnp.tile(_rope_table(past + jnp.arange(ts)), (bs, 1))
    x_p = x_prompt.reshape(n_p, d)
    x_s = x_sample.reshape(n_s, d)
    caches = (cache_mla_latent, cache_mla_rope, cache_fox_k, cache_fox_v, cache_fox_logf)
    st_p, st_s = [], []
    for i in range(depth):
        last = i == depth - 1
        lw = _layer_weights(w_in[i], b_forget[i], b_gate[i], w_q_up[i], w_uk[i], w_uv[i],
                            heads, ql, kvl)
        w_o_b = w_o[i].astype(BF)
        mi_p = _mixer_inputs(x_p, cs_p, lw, g_mix[i], g_q_lat[i], g_kv_lat[i], heads, ql, kvl, tm,
                             prompt_bt=(bp, tp))
        mi_s = _mixer_inputs(x_s, cs_s, lw, g_mix[i], g_q_lat[i], g_kv_lat[i], heads, ql, kvl, tm)
        oa_p, ob_p = _prompt_attention(mi_p, lw, bp, tp, heads, kvl, tm)
        oa_s, ob_s = _sample_attention(mi_s, lw, caches, i, bs, ts, heads, kvl, tm)
        x_p = _out_proj(x_p, oa_p, ob_p, mi_p['gates'], w_o_b)
        x_s = _out_proj(x_s, oa_s, ob_s, mi_s['gates'], w_o_b)
        xn_p, lg_p = _ffn_norm(x_p, g_ffn[i], w_router[i], b_router[i])
        xn_s, lg_s = _ffn_norm(x_s, g_ffn[i], w_router[i], b_router[i])
        ys, route_w = _moe(jnp.concatenate([xn_p, xn_s], axis=0),
                           jnp.concatenate([lg_p, lg_s], axis=0),
                           w_gu[i], b_gu[i], w_down[i], b_down[i], bm)
        w_plg_b = w_plg[i].astype(BF)
        w_pl_b = w_pl[i].astype(BF)
        x_p = _ple(x_p, ys, route_w, 0, p_prompt[i].reshape(n_p, -1), w_plg_b, b_plg[i], w_pl_b,
                   g_final, last)
        x_s = _ple(x_s, ys, route_w, n_p, p_sample[i].reshape(n_s, -1), w_plg_b, b_plg[i], w_pl_b,
                   g_final, last)
        for st, mi, b, t in ((st_p, mi_p, bp, tp), (st_s, mi_s, bs, ts)):
            st.append((mi['clat'].reshape(b, t, kvl),
                       mi['kr'][:, :ROPE].reshape(b, t, ROPE),
                       mi['fk'].reshape(b, t, heads, HEAD),
                       mi['fv'].reshape(b, t, heads, HEAD),
                       mi['logf'][:, :heads].reshape(b, t, heads)))
    outs = [x_p.reshape(bp, tp, d), x_s.reshape(bs, ts, d)]
    for st in (st_p, st_s):
        for k in range(5):
            outs.append(jnp.stack([s[k] for s in st]))
    return tuple(outs)
```

```python
import functools

import jax
import jax.numpy as jnp
from jax import lax
from jax.experimental import pallas as pl
from jax.experimental.pallas import tpu as pltpu

CHUNK = 64
NORM_EPS = 1e-6
HEAD = 128
ROPE = 64
QK_PAD = 256
MLA_SCALE = (HEAD + ROPE) ** -0.5
FOX_SCALE = HEAD ** -0.5
ROPE_THETA = 10000.0
TOP_K = 4
SWIGLU_LIMIT = 7.0
SWIGLU_ALPHA = 1.702
LANE = 128
VMEM_LIMIT = 56 * 1024 * 1024
NEG = -1e30
LOG2E = 1.4426950408889634
FLASH_BLOCK = 1024
FLASH_SUBBLOCKS = 2
PLE_ROWS = 256

BF = jnp.bfloat16
F32 = jnp.float32


def _tile(n, pref):
    if n <= pref:
        return n
    for t in range(pref, 7, -1):
        if n % t == 0 and t % 8 == 0:
            return t
    return n


def _params(sem):
    return pltpu.CompilerParams(dimension_semantics=sem, vmem_limit_bytes=VMEM_LIMIT)


def _rms(x, g):
    return x * lax.rsqrt(jnp.mean(x * x, axis=-1, keepdims=True) + NORM_EPS) * g


def _sigmoid(x):
    return 1.0 / (1.0 + jnp.exp(-x))


def _lanes(x, n):
    return jnp.tile(x, (1, n // LANE)) if n >= LANE else x[:, :n]


def _rmsnorm_kernel(x_ref, g_ref, o_ref):
    o_ref[...] = _rms(x_ref[...], g_ref[...]).astype(o_ref.dtype)


def _rmsnorm(x, g, out_dtype, tm=512):
    m, d = x.shape
    tm = _tile(m, tm)
    return pl.pallas_call(
        _rmsnorm_kernel,
        grid=(m // tm,),
        in_specs=[pl.BlockSpec((tm, d), lambda i: (i, 0)),
                  pl.BlockSpec((1, d), lambda i: (0, 0))],
        out_specs=pl.BlockSpec((tm, d), lambda i: (i, 0)),
        out_shape=jax.ShapeDtypeStruct((m, d), out_dtype),
        compiler_params=_params(("parallel",)),
        name="rmsnorm",
    )(x, g.reshape(1, d))


def _ffn_norm_kernel(x_ref, g_ref, wr_ref, br_ref, xn_ref, lg_ref):
    xn = _rms(x_ref[...], g_ref[...])
    xn_ref[...] = xn.astype(xn_ref.dtype)
    lg_ref[...] = jnp.dot(xn, wr_ref[...], preferred_element_type=F32,
                          precision=lax.Precision.HIGHEST) + br_ref[...]


def _ffn_norm(x, g, w_router, b_router, tm=512):
    m, d = x.shape
    e = w_router.shape[1]
    ep = -(-e // LANE) * LANE
    wr = jnp.pad(w_router, ((0, 0), (0, ep - e)))
    br = jnp.pad(b_router, (0, ep - e)).reshape(1, ep)
    tm = _tile(m, tm)
    xn, lg = pl.pallas_call(
        _ffn_norm_kernel,
        grid=(m // tm,),
        in_specs=[pl.BlockSpec((tm, d), lambda i: (i, 0)),
                  pl.BlockSpec((1, d), lambda i: (0, 0)),
                  pl.BlockSpec((d, ep), lambda i: (0, 0)),
                  pl.BlockSpec((1, ep), lambda i: (0, 0))],
        out_specs=[pl.BlockSpec((tm, d), lambda i: (i, 0)),
                   pl.BlockSpec((tm, ep), lambda i: (i, 0))],
        out_shape=[jax.ShapeDtypeStruct((m, d), BF),
                   jax.ShapeDtypeStruct((m, ep), F32)],
        compiler_params=_params(("parallel",)),
        name="ffn_norm_router",
    )(x, g.reshape(1, d), wr, br)
    return xn, lg[:, :e]


def _mm_kernel(*refs, n_extra, epilogue):
    a_ref, w_ref = refs[0], refs[1]
    extra = refs[2:2 + n_extra]
    outs = refs[2 + n_extra:]
    acc = jnp.dot(a_ref[...], w_ref[...], preferred_element_type=F32)
    epilogue(acc, extra, outs)


def _mm(a, w, epilogue, out_shapes, out_widths, *, tm, tn, extra=(), extra_specs=(),
        a_spec=None, w_spec=None, n_col_tiles=None, name="mm"):
    m = a.shape[0]
    k = a.shape[-1] if a_spec is None else None
    tm = _tile(m, tm)
    if n_col_tiles is None:
        n_col_tiles = w.shape[-1] // tn
    if a_spec is None:
        a_spec = pl.BlockSpec((tm, k), lambda i, j: (i, 0))
    if w_spec is None:
        w_spec = pl.BlockSpec((w.shape[0], tn), lambda i, j: (0, j))
    out_specs = [pl.BlockSpec((tm, ow), lambda i, j: (i, j)) for ow in out_widths]
    return pl.pallas_call(
        functools.partial(_mm_kernel, n_extra=len(extra), epilogue=epilogue),
        grid=(m // tm, n_col_tiles),
        in_specs=[a_spec, w_spec, *extra_specs],
        out_specs=out_specs,
        out_shape=out_shapes,
        compiler_params=_params(("parallel", "arbitrary")),
        name=name,
    )(a, w, *extra)


def _rope_mix(v, cs_ref):
    cs = cs_ref[...]
    lane = lax.broadcasted_iota(jnp.int32, cs.shape, 1)
    c = jnp.where(lane < ROPE, cs, 0.0)
    s = jnp.where(lane < ROPE, pltpu.roll(cs, ROPE, 1), 0.0)
    return v * c + pltpu.roll(v, ROPE, 1) * s


def _latent_epilogue(acc, extra, outs, *, ql, kvl):
    gq_ref, gkv_ref, bf_ref, cs_ref = extra
    cqn_ref, clat_ref, clatb_ref, kr_ref, krb_ref, logf_ref = outs
    cqn_ref[...] = _rms(acc[:, :ql], gq_ref[...]).astype(cqn_ref.dtype)
    clat = _rms(acc[:, ql:ql + kvl], gkv_ref[...])
    clat_ref[...] = clat
    clatb_ref[...] = clat.astype(clatb_ref.dtype)
    o = ql + kvl
    kr = _rope_mix(acc[:, o:o + LANE], cs_ref)
    kr_ref[...] = kr
    krb_ref[...] = kr.astype(krb_ref.dtype)
    fg = acc[:, o + LANE:o + 2 * LANE] + bf_ref[...]
    logf_ref[...] = jnp.minimum(fg, 0.0) - jnp.log(1.0 + jnp.exp(-jnp.abs(fg)))


def _scaled_bf16_epilogue(acc, extra, outs, *, scale):
    outs[0][...] = (acc * scale).astype(outs[0].dtype)


def _dual_epilogue(acc, extra, outs):
    outs[0][...] = acc
    outs[1][...] = acc.astype(outs[1].dtype)


def _gate_epilogue(acc, extra, outs):
    outs[0][...] = _sigmoid(acc + extra[0][...]).astype(outs[0].dtype)


def _q_epilogue(acc, extra, outs, *, heads_per_tile, scale):
    cs_ref = extra[0]
    o_ref = outs[0]
    for h in range(heads_per_tile):
        b = h * QK_PAD
        o_ref[:, b:b + HEAD] = (acc[:, b:b + HEAD] * scale).astype(o_ref.dtype)
        o_ref[:, b + HEAD:b + QK_PAD] = (
            _rope_mix(acc[:, b + HEAD:b + QK_PAD], cs_ref) * scale).astype(o_ref.dtype)


def _fox_q_aug_epilogue(acc, extra, outs, *, heads_per_tile, scale):
    o_ref = outs[0]
    lane = lax.broadcasted_iota(jnp.int32, (acc.shape[0], HEAD), 1)
    ones = jnp.where(lane < 3, 1.0, 0.0).astype(o_ref.dtype)
    for h in range(heads_per_tile):
        o_ref[:, h * QK_PAD:h * QK_PAD + HEAD] = (
            acc[:, h * HEAD:(h + 1) * HEAD] * scale).astype(o_ref.dtype)
        o_ref[:, h * QK_PAD + HEAD:(h + 1) * QK_PAD] = ones


def _fox_k_aug_epilogue(acc, extra, outs, *, heads_per_tile):
    ext_ref = extra[0]
    k_ref, o_ref = outs
    k_ref[...] = acc
    for h in range(heads_per_tile):
        o_ref[:, h * QK_PAD:h * QK_PAD + HEAD] = acc[:, h * HEAD:(h + 1) * HEAD].astype(o_ref.dtype)
        o_ref[:, h * QK_PAD + HEAD:(h + 1) * QK_PAD] = ext_ref[:, h * HEAD:(h + 1) * HEAD]


def _k_epilogue(acc, extra, outs, *, heads_per_tile):
    krb_ref = extra[0]
    o_ref = outs[0]
    for h in range(heads_per_tile):
        o_ref[:, h * QK_PAD:h * QK_PAD + HEAD] = acc[:, h * HEAD:(h + 1) * HEAD].astype(o_ref.dtype)
        o_ref[:, h * QK_PAD + HEAD:(h + 1) * QK_PAD] = krb_ref[...]


def _bf16_epilogue(acc, extra, outs):
    outs[0][...] = acc.astype(outs[0].dtype)


def _flash_prompt_kernel(q_ref, k_ref, v_ref, o_ref, m_sc, l_sc, acc_sc, *, t, nsub, causal):
    i = pl.program_id(2)
    m_sc[...] = jnp.full(m_sc.shape, NEG, F32)
    l_sc[...] = jnp.zeros(l_sc.shape, F32)
    acc_sc[...] = jnp.zeros(acc_sc.shape, F32)
    rep = t // LANE

    def load_kv(j):
        start = pl.multiple_of(j * t, t)
        return k_ref[pl.ds(start, t), :], v_ref[pl.ds(start, t), :]

    def update(u, k, v, masked):
        rows = slice(u * t, (u + 1) * t)
        s = lax.dot_general(q_ref[rows, :], k, (((1,), (1,)), ((), ())),
                            preferred_element_type=F32)
        if masked:
            qp = lax.broadcasted_iota(jnp.int32, s.shape, 0)
            kp = lax.broadcasted_iota(jnp.int32, s.shape, 1)
            keep = (kp <= qp) if causal else ((kp // CHUNK) <= (qp // CHUNK))
            s = jnp.where(keep, s, NEG)
        m_prev = m_sc[rows, :]
        m_new = jnp.maximum(m_prev, jnp.max(s, axis=-1, keepdims=True))
        alpha = jnp.exp2(m_prev - m_new)
        p = jnp.exp2(s - jnp.tile(m_new, (1, rep)))
        l_sc[rows, :] = alpha * l_sc[rows, :] + jnp.sum(p, axis=-1, keepdims=True)
        acc_sc[rows, :] = alpha * acc_sc[rows, :] + jnp.dot(p.astype(v.dtype), v,
                                                            preferred_element_type=F32)
        m_sc[rows, :] = m_new

    def body(j, carry):
        k, v = load_kv(j)
        for u in range(nsub):
            update(u, k, v, False)
        return carry

    lax.fori_loop(0, nsub * i, body, 0)
    for w in range(nsub):
        k, v = load_kv(nsub * i + w)
        for u in range(w, nsub):
            update(u, k, v, u == w)
    o_ref[...] = (acc_sc[...] / l_sc[...]).astype(o_ref.dtype)


def _flash_prompt(q, k, v, heads, causal, t=None, nsub=None):
    b, tt, _ = q.shape
    t = _tile(tt, FLASH_BLOCK if t is None else t)
    nsub = FLASH_SUBBLOCKS if nsub is None else nsub
    if tt % (nsub * t):
        nsub = 1
    tq = nsub * t
    return pl.pallas_call(
        functools.partial(_flash_prompt_kernel, t=t, nsub=nsub, causal=causal),
        grid=(b, heads, tt // tq),
        in_specs=[pl.BlockSpec((None, tq, QK_PAD), lambda b_, h, i: (b_, i, h)),
                  pl.BlockSpec((None, tt, QK_PAD), lambda b_, h, i: (b_, 0, h)),
                  pl.BlockSpec((None, tt, HEAD), lambda b_, h, i: (b_, 0, h))],
        out_specs=pl.BlockSpec((None, tq, HEAD), lambda b_, h, i: (b_, i, h)),
        out_shape=jax.ShapeDtypeStruct((b, tt, heads * HEAD), BF),
        scratch_shapes=[pltpu.VMEM((tq, LANE), F32), pltpu.VMEM((tq, LANE), F32),
                        pltpu.VMEM((tq, HEAD), F32)],
        compiler_params=_params(("parallel", "parallel", "arbitrary")),
        name="fox_prompt_attn" if causal else "mla_prompt_attn",
    )(q, k, v)


def _mla_sample_kernel(ql_ref, qr_ref, cc_ref, cr_ref, cn_ref, rn_ref, o_ref, m_sc, l_sc, acc_sc):
    j = pl.program_id(1)

    @pl.when(j == 0)
    def _():
        m_sc[...] = jnp.full(m_sc.shape, NEG, F32)
        l_sc[...] = jnp.zeros(l_sc.shape, F32)
        acc_sc[...] = jnp.zeros(acc_sc.shape, F32)

    ql = ql_ref[...]
    qr = qr_ref[...]

    def update(c, r):
        s = (lax.dot_general(ql, c, (((1,), (1,)), ((), ())), preferred_element_type=F32)
             + lax.dot_general(qr, r, (((1,), (1,)), ((), ())), preferred_element_type=F32))
        m_prev = m_sc[...]
        m_new = jnp.maximum(m_prev, jnp.max(s, axis=-1, keepdims=True))
        alpha = jnp.exp(m_prev - m_new)
        p = jnp.exp(s - _lanes(m_new, s.shape[1]))
        l_sc[...] = alpha * l_sc[...] + jnp.sum(p, axis=-1, keepdims=True)
        acc_sc[...] = _lanes(alpha, c.shape[1]) * acc_sc[...] + jnp.dot(
            p.astype(c.dtype), c, preferred_element_type=F32)
        m_sc[...] = m_new

    update(cc_ref[...].astype(BF), cr_ref[...].astype(BF))

    @pl.when(j == pl.num_programs(1) - 1)
    def _():
        update(cn_ref[...], rn_ref[...])
        o_ref[...] = (acc_sc[...] / _lanes(l_sc[...], o_ref.shape[1])).astype(o_ref.dtype)


def _mla_sample(q_lat, q_rope, cache_lat, cache_rope, c_new, r_new, tk=512):
    b, r, c = q_lat.shape
    p = cache_lat.shape[1]
    tn = c_new.shape[1]
    tk = _tile(p, tk)
    return pl.pallas_call(
        _mla_sample_kernel,
        grid=(b, p // tk),
        in_specs=[pl.BlockSpec((None, r, c), lambda b_, j: (b_, 0, 0)),
                  pl.BlockSpec((None, r, LANE), lambda b_, j: (b_, 0, 0)),
                  pl.BlockSpec((None, tk, c), lambda b_, j: (b_, j, 0)),
                  pl.BlockSpec((None, tk, LANE), lambda b_, j: (b_, j, 0)),
                  pl.BlockSpec((None, tn, c), lambda b_, j: (b_, 0, 0)),
                  pl.BlockSpec((None, tn, LANE), lambda b_, j: (b_, 0, 0))],
        out_specs=pl.BlockSpec((None, r, c), lambda b_, j: (b_, 0, 0)),
        out_shape=jax.ShapeDtypeStruct((b, r, c), BF),
        scratch_shapes=[pltpu.VMEM((r, LANE), F32), pltpu.VMEM((r, LANE), F32),
                        pltpu.VMEM((r, c), F32)],
        compiler_params=_params(("parallel", "arbitrary")),
        name="mla_sample_attn",
    )(q_lat, q_rope, cache_lat, cache_rope, c_new, r_new)


def _fox_sample_kernel(q_ref, ck_ref, cv_ref, kn_ref, vn_ref, cc_ref, cn_ref,
                       o_ref, m_sc, l_sc, acc_sc, *, heads):
    j = pl.program_id(1)
    t = q_ref.shape[0]
    head_cols = lambda h: slice(h * HEAD, (h + 1) * HEAD)
    head_rows = lambda h: slice(h * t, (h + 1) * t)

    @pl.when(j == 0)
    def _():
        m_sc[...] = jnp.full(m_sc.shape, NEG, F32)
        l_sc[...] = jnp.zeros(l_sc.shape, F32)
        acc_sc[...] = jnp.zeros(acc_sc.shape, F32)

    def update(keys, vals, neg_cum, causal):
        s = jnp.concatenate(
            [lax.dot_general(q_ref[:, head_cols(h)], keys(h), (((1,), (1,)), ((), ())),
                             preferred_element_type=F32) + neg_cum(h) for h in range(heads)],
            axis=0)
        if causal:
            qp = lax.broadcasted_iota(jnp.int32, s.shape, 0) % t
            kp = lax.broadcasted_iota(jnp.int32, s.shape, 1)
            s = jnp.where(kp <= qp, s, NEG)
        m_prev = m_sc[...]
        m_new = jnp.maximum(m_prev, jnp.max(s, axis=-1, keepdims=True))
        alpha = jnp.exp(m_prev - m_new)
        p = jnp.exp(s - _lanes(m_new, s.shape[1]))
        l_sc[...] = alpha * l_sc[...] + jnp.sum(p, axis=-1, keepdims=True)
        p = p.astype(BF)
        pv = jnp.concatenate(
            [jnp.dot(p[head_rows(h)], vals(h), preferred_element_type=F32) for h in range(heads)],
            axis=0)
        acc_sc[...] = alpha * acc_sc[...] + pv
        m_sc[...] = m_new

    tk = ck_ref.shape[0] // heads
    cache_rows = lambda h: pl.ds(h, tk, stride=heads)
    update(lambda h: ck_ref[cache_rows(h), :].astype(BF),
           lambda h: cv_ref[cache_rows(h), :].astype(BF),
           lambda h: cc_ref[h], False)

    @pl.when(j == pl.num_programs(1) - 1)
    def _():
        update(lambda h: kn_ref[:, head_cols(h)], lambda h: vn_ref[:, head_cols(h)],
               lambda h: cn_ref[h], True)
        o = acc_sc[...] / l_sc[...]
        for h in range(heads):
            o_ref[:, head_cols(h)] = o[head_rows(h)].astype(o_ref.dtype)


def _fox_sample(fq, fk_new, fv_new, cache_k, cache_v, layer, neg_cum_cache, neg_cum_new, heads,
                tk=512):
    b, t, d = fq.shape
    p = cache_k.shape[2]
    tk = _tile(p, tk)
    cache_spec = pl.BlockSpec((None, None, tk * heads, HEAD), lambda b_, j: (layer, b_, j, 0))
    flat = lambda c: c.reshape(c.shape[0], b, p * heads, HEAD)
    return pl.pallas_call(
        functools.partial(_fox_sample_kernel, heads=heads),
        grid=(b, p // tk),
        in_specs=[pl.BlockSpec((None, t, d), lambda b_, j: (b_, 0, 0)),
                  cache_spec, cache_spec,
                  pl.BlockSpec((None, t, d), lambda b_, j: (b_, 0, 0)),
                  pl.BlockSpec((None, t, d), lambda b_, j: (b_, 0, 0)),
                  pl.BlockSpec((None, heads, 1, tk), lambda b_, j: (b_, 0, 0, j)),
                  pl.BlockSpec((None, heads, 1, t), lambda b_, j: (b_, 0, 0, 0))],
        out_specs=pl.BlockSpec((None, t, d), lambda b_, j: (b_, 0, 0)),
        out_shape=jax.ShapeDtypeStruct((b, t, d), BF),
        scratch_shapes=[pltpu.VMEM((heads * t, LANE), F32), pltpu.VMEM((heads * t, LANE), F32),
                        pltpu.VMEM((heads * t, HEAD), F32)],
        compiler_params=_params(("parallel", "arbitrary")),
        name="fox_sample_attn",
    )(fq, flat(cache_k), flat(cache_v), fk_new, fv_new, neg_cum_cache, neg_cum_new)


def _out_proj_kernel(x_ref, oa_ref, ob_ref, ga_ref, gb_ref, w_ref, o_ref):
    mix = (ga_ref[...].astype(F32) * oa_ref[...].astype(F32)
           + gb_ref[...].astype(F32) * ob_ref[...].astype(F32)).astype(BF)
    o_ref[...] = x_ref[...] + jnp.dot(mix, w_ref[...], preferred_element_type=F32)


def _out_proj(x, o_a, o_b, gates, w_o, tm=256):
    m, d = x.shape
    tm = _tile(m, tm)
    row = lambda i: (i, 0)
    return pl.pallas_call(
        _out_proj_kernel,
        grid=(m // tm,),
        in_specs=[pl.BlockSpec((tm, d), row), pl.BlockSpec((tm, d), row),
                  pl.BlockSpec((tm, d), row),
                  pl.BlockSpec((tm, d), lambda i: (i, 0)),
                  pl.BlockSpec((tm, d), lambda i: (i, 1)),
                  pl.BlockSpec((d, d), lambda i: (0, 0))],
        out_specs=pl.BlockSpec((tm, d), row),
        out_shape=jax.ShapeDtypeStruct((m, d), F32),
        compiler_params=_params(("parallel",)),
        name="gated_out_proj",
    )(x, o_a, o_b, gates, gates, w_o)


def _new_weight_tile(be_ref):
    i = pl.program_id(1)
    return (i == 0) | (be_ref[i] != be_ref[jnp.maximum(i - 1, 0)])


def _moe_up_kernel(be_ref, nu_ref, x_ref, wg_ref, wu_ref, bg_ref, bu_ref, o_ref, wg_sc, wu_sc):
    used = pl.program_id(1) < nu_ref[0]

    @pl.when(used & _new_weight_tile(be_ref))
    def _():
        wg_sc[...] = wg_ref[...].astype(BF)
        wu_sc[...] = wu_ref[...].astype(BF)

    @pl.when(used)
    def _():
        x = x_ref[...]
        g = jnp.dot(x, wg_sc[...], preferred_element_type=F32) + bg_ref[...]
        u = jnp.dot(x, wu_sc[...], preferred_element_type=F32) + bu_ref[...]
        g = jnp.minimum(g, SWIGLU_LIMIT)
        u = jnp.clip(u, -SWIGLU_LIMIT, SWIGLU_LIMIT)
        o_ref[...] = (g * _sigmoid(SWIGLU_ALPHA * g) * (u + 1.0)).astype(o_ref.dtype)


def _moe_down_kernel(be_ref, nu_ref, a_ref, w_ref, b_ref, o_ref, w_sc):
    used = pl.program_id(1) < nu_ref[0]

    @pl.when(used & _new_weight_tile(be_ref))
    def _():
        w_sc[...] = w_ref[...].astype(BF)

    @pl.when(used)
    def _():
        o_ref[...] = (jnp.dot(a_ref[...], w_sc[...], preferred_element_type=F32)
                      + b_ref[...]).astype(o_ref.dtype)


def _moe_experts(xs, block_exp, n_used, w_gu, b_gu, w_down, b_down, bm, tf=512, tn=1024):
    p, d = xs.shape
    e, _, ff2 = w_gu.shape
    ff = ff2 // 2
    tf = _tile(ff, tf)
    tn = _tile(d, tn)
    nf = ff // tf
    nb = p // bm
    blk = lambda i, nu: jnp.minimum(i, nu[0] - 1)
    exp = lambda i, be, nu: be[blk(i, nu)]
    act = pl.pallas_call(
        _moe_up_kernel,
        grid_spec=pltpu.PrefetchScalarGridSpec(
            num_scalar_prefetch=2,
            grid=(nf, nb),
            in_specs=[pl.BlockSpec((bm, d), lambda f, i, be, nu: (blk(i, nu), 0)),
                      pl.BlockSpec((None, d, tf), lambda f, i, be, nu: (exp(i, be, nu), 0, f)),
                      pl.BlockSpec((None, d, tf), lambda f, i, be, nu: (exp(i, be, nu), 0, nf + f)),
                      pl.BlockSpec((None, 1, tf), lambda f, i, be, nu: (exp(i, be, nu), 0, f)),
                      pl.BlockSpec((None, 1, tf), lambda f, i, be, nu: (exp(i, be, nu), 0, nf + f))],
            out_specs=pl.BlockSpec((bm, tf), lambda f, i, be, nu: (blk(i, nu), f)),
            scratch_shapes=[pltpu.VMEM((d, tf), BF), pltpu.VMEM((d, tf), BF)],
        ),
        out_shape=jax.ShapeDtypeStruct((p, ff), BF),
        compiler_params=_params(("parallel", "arbitrary")),
        name="moe_gate_up",
    )(block_exp, n_used, xs, w_gu, w_gu, b_gu.reshape(e, 1, ff2), b_gu.reshape(e, 1, ff2))
    return pl.pallas_call(
        _moe_down_kernel,
        grid_spec=pltpu.PrefetchScalarGridSpec(
            num_scalar_prefetch=2,
            grid=(d // tn, nb),
            in_specs=[pl.BlockSpec((bm, ff), lambda n, i, be, nu: (blk(i, nu), 0)),
                      pl.BlockSpec((None, ff, tn), lambda n, i, be, nu: (exp(i, be, nu), 0, n)),
                      pl.BlockSpec((None, 1, tn), lambda n, i, be, nu: (exp(i, be, nu), 0, n))],
            out_specs=pl.BlockSpec((bm, tn), lambda n, i, be, nu: (blk(i, nu), n)),
            scratch_shapes=[pltpu.VMEM((ff, tn), BF)],
        ),
        out_shape=jax.ShapeDtypeStruct((p, d), BF),
        compiler_params=_params(("parallel", "arbitrary")),
        name="moe_down",
    )(block_exp, n_used, act, w_down, b_down.reshape(e, 1, d))


def _rows(x, idx):
    return x.at[idx].get(mode='promise_in_bounds', unique_indices=False)


def _moe(xn, logits, w_gu, b_gu, w_down, b_down, bm):
    n, d = xn.shape
    e = w_gu.shape[0]
    top_v, top_i = lax.top_k(logits, TOP_K)
    gates = jax.nn.softmax(top_v, axis=-1)
    nk = n * TOP_K
    a_exp = top_i.reshape(-1).astype(jnp.int32)
    onehot = (a_exp[:, None] == jnp.arange(e, dtype=jnp.int32)[None, :]).astype(jnp.int32)
    rank = jnp.sum((jnp.cumsum(onehot, axis=0) - onehot) * onehot, axis=1)
    counts = jnp.sum(onehot, axis=0)
    padded = (counts + bm - 1) // bm * bm
    pends = jnp.cumsum(padded)
    pstarts = pends - padded
    dest = pstarts[a_exp] + rank
    nb = -(-(nk + e * (bm - 1)) // bm)
    p = nb * bm
    a_tok = jnp.arange(nk, dtype=jnp.int32) // TOP_K
    buf_tok = jnp.full((p,), n, jnp.int32).at[dest].set(a_tok)
    block_start = jnp.arange(nb, dtype=jnp.int32) * bm
    block_exp = jnp.minimum(jnp.sum((pends[None, :] <= block_start[:, None]).astype(jnp.int32), axis=1),
                            e - 1)
    n_used = (pends[-1:] // bm).astype(jnp.int32)
    xpad = jnp.concatenate([xn, jnp.zeros((1, d), xn.dtype)], axis=0)
    xs = _rows(xpad, buf_tok)
    out = _moe_experts(xs, block_exp, n_used, w_gu, b_gu, w_down, b_down, bm)
    dest = dest.reshape(n, TOP_K)
    return [_rows(out, dest[:, k]) for k in range(TOP_K)], gates


def _ple_kernel(x_ref, y0_ref, y1_ref, y2_ref, y3_ref, rw_ref, p_ref, wg_ref, bg_ref, wp_ref,
                gf_ref, o_ref, *, final):
    rw = rw_ref[...]
    y = lambda r: r[...].astype(F32)
    x = x_ref[...] + ((rw[:, 0:1] * y(y0_ref) + rw[:, 1:2] * y(y1_ref))
                      + (rw[:, 2:3] * y(y2_ref) + rw[:, 3:4] * y(y3_ref)))
    gate = _sigmoid(jnp.dot(x.astype(BF), wg_ref[...], preferred_element_type=F32) + bg_ref[...])
    x = x + gate * jnp.dot(p_ref[...].astype(BF), wp_ref[...], preferred_element_type=F32)
    if final:
        x = _rms(x, gf_ref[...])
    o_ref[...] = x


def _ple(x, p_emb, x_row0, m, ys, route_w, y_row0, w_plg, b_plg, w_pl, g_final, final, tm=256):
    d = x.shape[1]
    pd = p_emb.shape[1]
    tm = _tile(m, tm)
    xr, yr = x_row0 // tm, y_row0 // tm
    xrow = lambda i: (i + xr, 0)
    yrow = lambda i: (i + yr, 0)
    fixed = lambda i: (0, 0)
    return pl.pallas_call(
        functools.partial(_ple_kernel, final=final),
        grid=(m // tm,),
        in_specs=[pl.BlockSpec((tm, d), xrow)] + [pl.BlockSpec((tm, d), yrow)] * TOP_K
                 + [pl.BlockSpec((tm, TOP_K), yrow), pl.BlockSpec((tm, pd), xrow),
                    pl.BlockSpec((d, d), fixed), pl.BlockSpec((1, d), fixed),
                    pl.BlockSpec((pd, d), fixed), pl.BlockSpec((1, d), fixed)],
        out_specs=pl.BlockSpec((tm, d), lambda i: (i, 0)),
        out_shape=jax.ShapeDtypeStruct((m, d), F32),
        compiler_params=_params(("parallel",)),
        name="ple_gate_final",
    )(x, *ys, route_w, p_emb, w_plg, b_plg.reshape(1, d), w_pl, g_final.reshape(1, d))


def _rot_cols(w):
    half = w.shape[-1] // 2
    return jnp.concatenate([-w[..., half:], w[..., :half]], axis=-1)


def _rope_table(pos):
    half = ROPE // 2
    freqs = ROPE_THETA ** (-jnp.arange(half, dtype=F32) / half)
    ang = pos.astype(F32)[:, None] * freqs[None, :]
    cos, sin = jnp.cos(ang), jnp.sin(ang)
    return jnp.concatenate([cos, cos, sin, sin], axis=-1)


def _layer_weights(w_in, b_forget, b_gate, w_q_up, w_uk, w_uv, heads, ql, kvl):
    d = w_in.shape[0]
    o = 0
    parts = []
    for s in (ql, kvl, ROPE, d, d, d, heads, 2 * d):
        parts.append(w_in[:, o:o + s])
        o += s
    w_cq, w_ckv, w_kr, w_fq, w_fk, w_fv, w_fg, w_gate = parts
    w_lat = jnp.concatenate(
        [w_cq, w_ckv, w_kr, _rot_cols(w_kr), jnp.pad(w_fg, ((0, 0), (0, LANE - heads)))], axis=1)
    wq_nope = w_q_up[:, :, :HEAD]
    wq_rope = w_q_up[:, :, HEAD:]
    wq = jnp.concatenate([wq_nope, wq_rope, _rot_cols(wq_rope)], axis=-1).reshape(ql, heads * QK_PAD)
    return dict(
        w_lat=w_lat.astype(BF), w_fq=w_fq.astype(BF), w_fk=w_fk.astype(BF), w_fv=w_fv.astype(BF),
        w_gate=w_gate.astype(BF), wq=wq.astype(BF),
        w_uk=w_uk.reshape(kvl, heads * HEAD).astype(BF),
        w_uv=w_uv.reshape(kvl, heads * HEAD).astype(BF),
        w_uk_t=jnp.transpose(w_uk, (1, 2, 0)).astype(BF),
        w_uv_h=jnp.transpose(w_uv, (1, 0, 2)).astype(BF),
        b_forget=jnp.pad(b_forget, (0, LANE - heads)).reshape(1, LANE),
        b_gate=b_gate.reshape(1, 2 * d))


def _trunc_bf16(x):
    bits = lax.bitcast_convert_type(x, jnp.uint32) & jnp.uint32(0xFFFF0000)
    return lax.bitcast_convert_type(bits, F32)


def _forget_bias_columns(logf, b, t, heads):
    cum = jnp.cumsum(logf.reshape(b, t, heads), axis=1)
    nck = -LOG2E * cum
    hi = _trunc_bf16(nck)
    mid = _trunc_bf16(nck - hi)
    lo = nck - hi - mid
    ext = jnp.stack([hi, mid, lo], axis=-1).astype(BF)
    ext = jnp.pad(ext, ((0, 0), (0, 0), (0, 0), (0, HEAD - 3)))
    return ext.reshape(b * t, heads * HEAD)


def _mixer_inputs(x, cs, lw, g_mix, g_q_lat, g_kv_lat, heads, ql, kvl, tm, prompt_bt=None):
    n, d = x.shape
    prompt = prompt_bt is not None
    h = _rmsnorm(x, g_mix, BF)
    tm = _tile(n, tm)
    row = lambda i, j: (i, 0)
    fixed = lambda i, j: (0, 0)
    wl = ql + kvl + 2 * LANE
    cqn, clat, clatb, kr, krb, logf = _mm(
        h, lw['w_lat'], functools.partial(_latent_epilogue, ql=ql, kvl=kvl),
        [jax.ShapeDtypeStruct((n, ql), BF), jax.ShapeDtypeStruct((n, kvl), F32),
         jax.ShapeDtypeStruct((n, kvl), BF), jax.ShapeDtypeStruct((n, LANE), F32),
         jax.ShapeDtypeStruct((n, LANE), BF), jax.ShapeDtypeStruct((n, LANE), F32)],
        [ql, kvl, kvl, LANE, LANE, LANE], tm=tm, tn=wl,
        extra=(g_q_lat.reshape(1, ql), g_kv_lat.reshape(1, kvl), lw['b_forget'], cs),
        extra_specs=(pl.BlockSpec((1, ql), fixed), pl.BlockSpec((1, kvl), fixed),
                     pl.BlockSpec((1, LANE), fixed), pl.BlockSpec((tm, LANE), row)),
        name="latent_proj")
    tn = _tile(d, 512)
    hpt = tn // HEAD
    if prompt:
        (fq,) = _mm(h, lw['w_fq'],
                    functools.partial(_fox_q_aug_epilogue, heads_per_tile=hpt,
                                      scale=FOX_SCALE * LOG2E),
                    [jax.ShapeDtypeStruct((n, heads * QK_PAD), BF)], [hpt * QK_PAD],
                    tm=tm, tn=tn, name="fox_q_proj")
        ext = _forget_bias_columns(logf[:, :heads], *prompt_bt, heads)
        fk, fkb = _mm(h, lw['w_fk'], functools.partial(_fox_k_aug_epilogue, heads_per_tile=hpt),
                      [jax.ShapeDtypeStruct((n, d), F32),
                       jax.ShapeDtypeStruct((n, heads * QK_PAD), BF)],
                      [tn, hpt * QK_PAD], tm=tm, tn=tn, extra=(ext,),
                      extra_specs=(pl.BlockSpec((tm, tn), lambda i, j: (i, j)),),
                      name="fox_k_proj")
    else:
        (fq,) = _mm(h, lw['w_fq'], functools.partial(_scaled_bf16_epilogue, scale=FOX_SCALE),
                    [jax.ShapeDtypeStruct((n, d), BF)], [tn], tm=tm, tn=tn, name="fox_q_proj")
        fk, fkb = _mm(h, lw['w_fk'], _dual_epilogue,
                      [jax.ShapeDtypeStruct((n, d), F32), jax.ShapeDtypeStruct((n, d), BF)],
                      [tn, tn], tm=tm, tn=tn, name="fox_k_proj")
    fv, fvb = _mm(h, lw['w_fv'], _dual_epilogue,
                  [jax.ShapeDtypeStruct((n, d), F32), jax.ShapeDtypeStruct((n, d), BF)],
                  [tn, tn], tm=tm, tn=tn, name="fox_v_proj")
    (gates,) = _mm(h, lw['w_gate'], _gate_epilogue, [jax.ShapeDtypeStruct((n, 2 * d), BF)], [tn],
                   tm=tm, tn=tn, extra=(lw['b_gate'],),
                   extra_specs=(pl.BlockSpec((1, tn), lambda i, j: (0, j)),), name="merge_gates")
    qh = 2 if heads % 2 == 0 else 1
    (q,) = _mm(cqn, lw['wq'],
               functools.partial(_q_epilogue, heads_per_tile=qh,
                                 scale=MLA_SCALE * LOG2E if prompt else MLA_SCALE),
               [jax.ShapeDtypeStruct((n, heads * QK_PAD), BF)], [qh * QK_PAD],
               tm=tm, tn=qh * QK_PAD, extra=(cs,),
               extra_specs=(pl.BlockSpec((tm, LANE), row),), name="mla_q_up")
    return dict(q=q, clat=clat, clatb=clatb, kr=kr, krb=krb, logf=logf, fq=fq, fk=fk, fkb=fkb,
                fv=fv, fvb=fvb, gates=gates)


def _prompt_attention(mi, lw, b, t, heads, kvl, tm):
    n = b * t
    d = heads * HEAD
    hpt = 2 if heads % 2 == 0 else 1
    tm = _tile(n, tm)
    (k,) = _mm(mi['clatb'], lw['w_uk'], functools.partial(_k_epilogue, heads_per_tile=hpt),
               [jax.ShapeDtypeStruct((n, heads * QK_PAD), BF)], [hpt * QK_PAD],
               tm=tm, tn=hpt * HEAD, extra=(mi['krb'],),
               extra_specs=(pl.BlockSpec((tm, LANE), lambda i, j: (i, 0)),), name="mla_k_up")
    (v,) = _mm(mi['clatb'], lw['w_uv'], _bf16_epilogue, [jax.ShapeDtypeStruct((n, d), BF)],
               [_tile(d, 512)], tm=tm, tn=_tile(d, 512), name="mla_v_up")
    o_a = _flash_prompt(mi['q'].reshape(b, t, -1), k.reshape(b, t, -1), v.reshape(b, t, -1),
                        heads, causal=False)
    o_b = _flash_prompt(mi['fq'].reshape(b, t, -1), mi['fkb'].reshape(b, t, -1),
                        mi['fvb'].reshape(b, t, -1), heads, causal=True)
    return o_a.reshape(n, d), o_b.reshape(n, d)


def _sample_attention(mi, lw, caches, layer, b, t, heads, kvl, tm):
    cache_lat, cache_rope, cache_k, cache_v, cache_logf = caches
    n = b * t
    d = heads * HEAD
    p = cache_lat.shape[2]
    tm = _tile(n, tm)
    (q_lat,) = _mm(mi['q'], lw['w_uk_t'], _bf16_epilogue,
                   [jax.ShapeDtypeStruct((n, heads * kvl), BF)], [kvl], tm=tm, tn=kvl,
                   a_spec=pl.BlockSpec((tm, HEAD), lambda i, j: (i, 2 * j)),
                   w_spec=pl.BlockSpec((None, HEAD, kvl), lambda i, j: (j, 0, 0)),
                   n_col_tiles=heads, name="mla_q_absorb")
    q_rope = mi['q'].reshape(n, heads, QK_PAD)[:, :, HEAD:]
    cache_rope_p = jnp.pad(cache_rope[layer], ((0, 0), (0, 0), (0, LANE - ROPE)))
    o_lat = _mla_sample(q_lat.reshape(b, t * heads, kvl), q_rope.reshape(b, t * heads, LANE),
                        cache_lat[layer], cache_rope_p, mi['clatb'].reshape(b, t, kvl),
                        mi['krb'].reshape(b, t, LANE))
    (o_a,) = _mm(o_lat.reshape(n, heads * kvl), lw['w_uv_h'], _bf16_epilogue,
                 [jax.ShapeDtypeStruct((n, d), BF)], [HEAD], tm=tm, tn=HEAD,
                 a_spec=pl.BlockSpec((tm, kvl), lambda i, j: (i, j)),
                 w_spec=pl.BlockSpec((None, kvl, HEAD), lambda i, j: (j, 0, 0)),
                 n_col_tiles=heads, name="mla_v_absorb")
    logf = mi['logf'][:, :heads].reshape(b, t, heads)
    cum = jnp.cumsum(jnp.concatenate([cache_logf[layer].astype(F32), logf], axis=1), axis=1)
    neg_cum = -cum.transpose(0, 2, 1)
    o_b = _fox_sample(mi['fq'].reshape(b, t, d), mi['fkb'].reshape(b, t, d),
                      mi['fvb'].reshape(b, t, d), cache_k, cache_v, layer,
                      neg_cum[:, :, None, :p], neg_cum[:, :, None, p:], heads)
    return o_a, o_b.reshape(n, d)


def kernel(x_prompt, x_sample, p_prompt, p_sample, cache_mla_latent, cache_mla_rope, cache_fox_k,
           cache_fox_v, cache_fox_logf, g_mix, w_in, b_forget, b_gate, g_q_lat, w_q_up, g_kv_lat,
           w_uk, w_uv, w_o, g_ffn, w_router, b_router, w_gu, b_gu, w_down, b_down, w_pl, w_plg,
           b_plg, g_final):
    bp, tp, d = x_prompt.shape
    bs, ts, _ = x_sample.shape
    depth = g_mix.shape[0]
    heads = d // HEAD
    ql = w_q_up.shape[1]
    kvl = w_uk.shape[1]
    past = cache_mla_latent.shape[2]
    n_p, n_s = bp * tp, bs * ts
    tm = 1024
    bm = 512

    cs_p = jnp.tile(_rope_table(jnp.arange(tp)), (bp, 1))
    cs_s = jnp.tile(_rope_table(past + jnp.arange(ts)), (bs, 1))
    x_p = x_prompt.reshape(n_p, d)
    x_s = x_sample.reshape(n_s, d)
    caches = (cache_mla_latent, cache_mla_rope, cache_fox_k, cache_fox_v, cache_fox_logf)
    st_p, st_s = [], []
    for i in range(depth):
        last = i == depth - 1
        lw = _layer_weights(w_in[i], b_forget[i], b_gate[i], w_q_up[i], w_uk[i], w_uv[i],
                            heads, ql, kvl)
        w_o_b = w_o[i].astype(BF)
        mi_p = _mixer_inputs(x_p, cs_p, lw, g_mix[i], g_q_lat[i], g_kv_lat[i], heads, ql, kvl, tm,
                             prompt_bt=(bp, tp))
        mi_s = _mixer_inputs(x_s, cs_s, lw, g_mix[i], g_q_lat[i], g_kv_lat[i], heads, ql, kvl, tm)
        oa_p, ob_p = _prompt_attention(mi_p, lw, bp, tp, heads, kvl, tm)
        oa_s, ob_s = _sample_attention(mi_s, lw, caches, i, bs, ts, heads, kvl, tm)
        x_p = _out_proj(x_p, oa_p, ob_p, mi_p['gates'], w_o_b)
        x_s = _out_proj(x_s, oa_s, ob_s, mi_s['gates'], w_o_b)
        xn_p, lg_p = _ffn_norm(x_p, g_ffn[i], w_router[i], b_router[i])
        xn_s, lg_s = _ffn_norm(x_s, g_ffn[i], w_router[i], b_router[i])
        split = (n_p // 2) // PLE_ROWS * PLE_ROWS
        w_plg_b = w_plg[i].astype(BF)
        w_pl_b = w_pl[i].astype(BF)
        pp = p_prompt[i].reshape(n_p, -1)
        ps = p_sample[i].reshape(n_s, -1)
        experts = lambda xn, lg: _moe(xn, lg, w_gu[i], b_gu[i], w_down[i], b_down[i], bm)
        ple = lambda x, pe, x0, m, ys, rw, y0: _ple(x, pe, x0, m, ys, rw, y0, w_plg_b, b_plg[i],
                                                    w_pl_b, g_final, last, tm=PLE_ROWS)
        ys_b, rw_b = experts(jnp.concatenate([xn_p[split:], xn_s], axis=0),
                             jnp.concatenate([lg_p[split:], lg_s], axis=0))
        if split:
            ys_a, rw_a = experts(xn_p[:split], lg_p[:split])
            x_p = jnp.concatenate([ple(x_p, pp, 0, split, ys_a, rw_a, 0),
                                   ple(x_p, pp, split, n_p - split, ys_b, rw_b, 0)], axis=0)
        else:
            x_p = ple(x_p, pp, 0, n_p, ys_b, rw_b, 0)
        x_s = ple(x_s, ps, 0, n_s, ys_b, rw_b, n_p - split)
        for st, mi, b, t in ((st_p, mi_p, bp, tp), (st_s, mi_s, bs, ts)):
            st.append((mi['clat'].reshape(b, t, kvl),
                       mi['kr'][:, :ROPE].reshape(b, t, ROPE),
                       mi['fk'].reshape(b, t, heads, HEAD),
                       mi['fv'].reshape(b, t, heads, HEAD),
                       mi['logf'][:, :heads].reshape(b, t, heads)))
    outs = [x_p.reshape(bp, tp, d), x_s.reshape(bs, ts, d)]
    for st in (st_p, st_s):
        for k in range(5):
            outs.append(jnp.stack([s[k] for s in st]))
    return tuple(outs)
```

```python
import functools

import jax
import jax.numpy as jnp
from jax import lax
from jax.experimental import pallas as pl
from jax.experimental.pallas import tpu as pltpu

CHUNK = 64
NORM_EPS = 1e-6
HEAD = 128
ROPE = 64
QK_PAD = 256
MLA_SCALE = (HEAD + ROPE) ** -0.5
FOX_SCALE = HEAD ** -0.5
ROPE_THETA = 10000.0
TOP_K = 4
SWIGLU_LIMIT = 7.0
SWIGLU_ALPHA = 1.702
LANE = 128
VMEM_LIMIT = 56 * 1024 * 1024
NEG = -1e30
LOG2E = 1.4426950408889634
FLASH_BLOCK = 1024
FLASH_SUBBLOCKS = 2
PLE_ROWS = 256

BF = jnp.bfloat16
F32 = jnp.float32


def _tile(n, pref):
    if n <= pref:
        return n
    for t in range(pref, 7, -1):
        if n % t == 0 and t % 8 == 0:
            return t
    return n


def _params(sem):
    return pltpu.CompilerParams(dimension_semantics=sem, vmem_limit_bytes=VMEM_LIMIT)


def _rms(x, g):
    return x * lax.rsqrt(jnp.mean(x * x, axis=-1, keepdims=True) + NORM_EPS) * g


def _sigmoid(x):
    return 1.0 / (1.0 + jnp.exp(-x))


def _lanes(x, n):
    return jnp.tile(x, (1, n // LANE)) if n >= LANE else x[:, :n]


def _rmsnorm_kernel(x_ref, g_ref, o_ref):
    o_ref[...] = _rms(x_ref[...], g_ref[...]).astype(o_ref.dtype)


def _rmsnorm(x, g, out_dtype, tm=512):
    m, d = x.shape
    tm = _tile(m, tm)
    return pl.pallas_call(
        _rmsnorm_kernel,
        grid=(m // tm,),
        in_specs=[pl.BlockSpec((tm, d), lambda i: (i, 0)),
                  pl.BlockSpec((1, d), lambda i: (0, 0))],
        out_specs=pl.BlockSpec((tm, d), lambda i: (i, 0)),
        out_shape=jax.ShapeDtypeStruct((m, d), out_dtype),
        compiler_params=_params(("parallel",)),
        name="rmsnorm",
    )(x, g.reshape(1, d))


def _ffn_norm_kernel(x_ref, g_ref, wr_ref, br_ref, xn_ref, lg_ref):
    xn = _rms(x_ref[...], g_ref[...])
    xn_ref[...] = xn.astype(xn_ref.dtype)
    lg_ref[...] = jnp.dot(xn, wr_ref[...], preferred_element_type=F32,
                          precision=lax.Precision.HIGHEST) + br_ref[...]


def _ffn_norm(x, g, w_router, b_router, tm=512):
    m, d = x.shape
    e = w_router.shape[1]
    ep = -(-e // LANE) * LANE
    wr = jnp.pad(w_router, ((0, 0), (0, ep - e)))
    br = jnp.pad(b_router, (0, ep - e)).reshape(1, ep)
    tm = _tile(m, tm)
    xn, lg = pl.pallas_call(
        _ffn_norm_kernel,
        grid=(m // tm,),
        in_specs=[pl.BlockSpec((tm, d), lambda i: (i, 0)),
                  pl.BlockSpec((1, d), lambda i: (0, 0)),
                  pl.BlockSpec((d, ep), lambda i: (0, 0)),
                  pl.BlockSpec((1, ep), lambda i: (0, 0))],
        out_specs=[pl.BlockSpec((tm, d), lambda i: (i, 0)),
                   pl.BlockSpec((tm, ep), lambda i: (i, 0))],
        out_shape=[jax.ShapeDtypeStruct((m, d), BF),
                   jax.ShapeDtypeStruct((m, ep), F32)],
        compiler_params=_params(("parallel",)),
        name="ffn_norm_router",
    )(x, g.reshape(1, d), wr, br)
    return xn, lg[:, :e]


def _mm_kernel(*refs, n_extra, epilogue):
    a_ref, w_ref = refs[0], refs[1]
    extra = refs[2:2 + n_extra]
    outs = refs[2 + n_extra:]
    acc = jnp.dot(a_ref[...], w_ref[...], preferred_element_type=F32)
    epilogue(acc, extra, outs)


def _mm(a, w, epilogue, out_shapes, out_widths, *, tm, tn, extra=(), extra_specs=(),
        a_spec=None, w_spec=None, n_col_tiles=None, name="mm"):
    m = a.shape[0]
    k = a.shape[-1] if a_spec is None else None
    tm = _tile(m, tm)
    if n_col_tiles is None:
        n_col_tiles = w.shape[-1] // tn
    if a_spec is None:
        a_spec = pl.BlockSpec((tm, k), lambda i, j: (i, 0))
    if w_spec is None:
        w_spec = pl.BlockSpec((w.shape[0], tn), lambda i, j: (0, j))
    out_specs = [pl.BlockSpec((tm, ow), lambda i, j: (i, j)) for ow in out_widths]
    return pl.pallas_call(
        functools.partial(_mm_kernel, n_extra=len(extra), epilogue=epilogue),
        grid=(m // tm, n_col_tiles),
        in_specs=[a_spec, w_spec, *extra_specs],
        out_specs=out_specs,
        out_shape=out_shapes,
        compiler_params=_params(("parallel", "arbitrary")),
        name=name,
    )(a, w, *extra)


def _rope_mix(v, cs_ref):
    cs = cs_ref[...]
    lane = lax.broadcasted_iota(jnp.int32, cs.shape, 1)
    c = jnp.where(lane < ROPE, cs, 0.0)
    s = jnp.where(lane < ROPE, pltpu.roll(cs, ROPE, 1), 0.0)
    return v * c + pltpu.roll(v, ROPE, 1) * s


def _latent_epilogue(acc, extra, outs, *, ql, kvl):
    gq_ref, gkv_ref, bf_ref, cs_ref = extra
    cqn_ref, clat_ref, clatb_ref, kr_ref, krb_ref, logf_ref = outs
    cqn_ref[...] = _rms(acc[:, :ql], gq_ref[...]).astype(cqn_ref.dtype)
    clat = _rms(acc[:, ql:ql + kvl], gkv_ref[...])
    clat_ref[...] = clat
    clatb_ref[...] = clat.astype(clatb_ref.dtype)
    o = ql + kvl
    kr = _rope_mix(acc[:, o:o + LANE], cs_ref)
    kr_ref[...] = kr
    krb_ref[...] = kr.astype(krb_ref.dtype)
    fg = acc[:, o + LANE:o + 2 * LANE] + bf_ref[...]
    logf_ref[...] = jnp.minimum(fg, 0.0) - jnp.log(1.0 + jnp.exp(-jnp.abs(fg)))


def _scaled_bf16_epilogue(acc, extra, outs, *, scale):
    outs[0][...] = (acc * scale).astype(outs[0].dtype)


def _dual_epilogue(acc, extra, outs):
    outs[0][...] = acc
    outs[1][...] = acc.astype(outs[1].dtype)


def _gate_epilogue(acc, extra, outs):
    outs[0][...] = _sigmoid(acc + extra[0][...]).astype(outs[0].dtype)


def _q_epilogue(acc, extra, outs, *, heads_per_tile, scale):
    cs_ref = extra[0]
    o_ref = outs[0]
    for h in range(heads_per_tile):
        b = h * QK_PAD
        o_ref[:, b:b + HEAD] = (acc[:, b:b + HEAD] * scale).astype(o_ref.dtype)
        o_ref[:, b + HEAD:b + QK_PAD] = (
            _rope_mix(acc[:, b + HEAD:b + QK_PAD], cs_ref) * scale).astype(o_ref.dtype)


def _fox_q_aug_epilogue(acc, extra, outs, *, heads_per_tile, scale):
    o_ref = outs[0]
    lane = lax.broadcasted_iota(jnp.int32, (acc.shape[0], HEAD), 1)
    ones = jnp.where(lane < 3, 1.0, 0.0).astype(o_ref.dtype)
    for h in range(heads_per_tile):
        o_ref[:, h * QK_PAD:h * QK_PAD + HEAD] = (
            acc[:, h * HEAD:(h + 1) * HEAD] * scale).astype(o_ref.dtype)
        o_ref[:, h * QK_PAD + HEAD:(h + 1) * QK_PAD] = ones


def _fox_k_aug_epilogue(acc, extra, outs, *, heads_per_tile):
    ext_ref = extra[0]
    k_ref, o_ref = outs
    k_ref[...] = acc
    for h in range(heads_per_tile):
        o_ref[:, h * QK_PAD:h * QK_PAD + HEAD] = acc[:, h * HEAD:(h + 1) * HEAD].astype(o_ref.dtype)
        o_ref[:, h * QK_PAD + HEAD:(h + 1) * QK_PAD] = ext_ref[:, h * HEAD:(h + 1) * HEAD]


def _k_epilogue(acc, extra, outs, *, heads_per_tile):
    krb_ref = extra[0]
    o_ref = outs[0]
    for h in range(heads_per_tile):
        o_ref[:, h * QK_PAD:h * QK_PAD + HEAD] = acc[:, h * HEAD:(h + 1) * HEAD].astype(o_ref.dtype)
        o_ref[:, h * QK_PAD + HEAD:(h + 1) * QK_PAD] = krb_ref[...]


def _bf16_epilogue(acc, extra, outs):
    outs[0][...] = acc.astype(outs[0].dtype)


def _flash_prompt_kernel(q_ref, k_ref, v_ref, o_ref, m_sc, l_sc, acc_sc, *, t, nsub, causal):
    i = pl.program_id(2)
    m_sc[...] = jnp.full(m_sc.shape, NEG, F32)
    l_sc[...] = jnp.zeros(l_sc.shape, F32)
    acc_sc[...] = jnp.zeros(acc_sc.shape, F32)
    rep = t // LANE

    def load_kv(j):
        start = pl.multiple_of(j * t, t)
        return k_ref[pl.ds(start, t), :], v_ref[pl.ds(start, t), :]

    def update(u, k, v, masked):
        rows = slice(u * t, (u + 1) * t)
        s = lax.dot_general(q_ref[rows, :], k, (((1,), (1,)), ((), ())),
                            preferred_element_type=F32)
        if masked:
            qp = lax.broadcasted_iota(jnp.int32, s.shape, 0)
            kp = lax.broadcasted_iota(jnp.int32, s.shape, 1)
            keep = (kp <= qp) if causal else ((kp // CHUNK) <= (qp // CHUNK))
            s = jnp.where(keep, s, NEG)
        m_prev = m_sc[rows, :]
        m_new = jnp.maximum(m_prev, jnp.max(s, axis=-1, keepdims=True))
        alpha = jnp.exp2(m_prev - m_new)
        p = jnp.exp2(s - jnp.tile(m_new, (1, rep)))
        l_sc[rows, :] = alpha * l_sc[rows, :] + jnp.sum(p, axis=-1, keepdims=True)
        acc_sc[rows, :] = alpha * acc_sc[rows, :] + jnp.dot(p.astype(v.dtype), v,
                                                            preferred_element_type=F32)
        m_sc[rows, :] = m_new

    def body(j, carry):
        k, v = load_kv(j)
        for u in range(nsub):
            update(u, k, v, False)
        return carry

    lax.fori_loop(0, nsub * i, body, 0)
    for w in range(nsub):
        k, v = load_kv(nsub * i + w)
        for u in range(w, nsub):
            update(u, k, v, u == w)
    o_ref[...] = (acc_sc[...] / l_sc[...]).astype(o_ref.dtype)


def _flash_prompt(q, k, v, heads, causal, t=None, nsub=None):
    b, tt, _ = q.shape
    t = _tile(tt, FLASH_BLOCK if t is None else t)
    nsub = FLASH_SUBBLOCKS if nsub is None else nsub
    if tt % (nsub * t):
        nsub = 1
    tq = nsub * t
    return pl.pallas_call(
        functools.partial(_flash_prompt_kernel, t=t, nsub=nsub, causal=causal),
        grid=(b, heads, tt // tq),
        in_specs=[pl.BlockSpec((None, tq, QK_PAD), lambda b_, h, i: (b_, i, h)),
                  pl.BlockSpec((None, tt, QK_PAD), lambda b_, h, i: (b_, 0, h)),
                  pl.BlockSpec((None, tt, HEAD), lambda b_, h, i: (b_, 0, h))],
        out_specs=pl.BlockSpec((None, tq, HEAD), lambda b_, h, i: (b_, i, h)),
        out_shape=jax.ShapeDtypeStruct((b, tt, heads * HEAD), BF),
        scratch_shapes=[pltpu.VMEM((tq, LANE), F32), pltpu.VMEM((tq, LANE), F32),
                        pltpu.VMEM((tq, HEAD), F32)],
        compiler_params=_params(("parallel", "parallel", "arbitrary")),
        name="fox_prompt_attn" if causal else "mla_prompt_attn",
    )(q, k, v)


def _mla_sample_kernel(ql_ref, qr_ref, cc_ref, cr_ref, cn_ref, rn_ref, o_ref, m_sc, l_sc, acc_sc):
    j = pl.program_id(1)

    @pl.when(j == 0)
    def _():
        m_sc[...] = jnp.full(m_sc.shape, NEG, F32)
        l_sc[...] = jnp.zeros(l_sc.shape, F32)
        acc_sc[...] = jnp.zeros(acc_sc.shape, F32)

    ql = ql_ref[...]
    qr = qr_ref[...]

    def update(c, r):
        s = (lax.dot_general(ql, c, (((1,), (1,)), ((), ())), preferred_element_type=F32)
             + lax.dot_general(qr, r, (((1,), (1,)), ((), ())), preferred_element_type=F32))
        m_prev = m_sc[...]
        m_new = jnp.maximum(m_prev, jnp.max(s, axis=-1, keepdims=True))
        alpha = jnp.exp(m_prev - m_new)
        p = jnp.exp(s - _lanes(m_new, s.shape[1]))
        l_sc[...] = alpha * l_sc[...] + jnp.sum(p, axis=-1, keepdims=True)
        acc_sc[...] = _lanes(alpha, c.shape[1]) * acc_sc[...] + jnp.dot(
            p.astype(c.dtype), c, preferred_element_type=F32)
        m_sc[...] = m_new

    update(cc_ref[...].astype(BF), cr_ref[...].astype(BF))

    @pl.when(j == pl.num_programs(1) - 1)
    def _():
        update(cn_ref[...], rn_ref[...])
        o_ref[...] = (acc_sc[...] / _lanes(l_sc[...], o_ref.shape[1])).astype(o_ref.dtype)


def _mla_sample(q_lat, q_rope, cache_lat, cache_rope, c_new, r_new, tk=512):
    b, r, c = q_lat.shape
    p = cache_lat.shape[1]
    tn = c_new.shape[1]
    tk = _tile(p, tk)
    return pl.pallas_call(
        _mla_sample_kernel,
        grid=(b, p // tk),
        in_specs=[pl.BlockSpec((None, r, c), lambda b_, j: (b_, 0, 0)),
                  pl.BlockSpec((None, r, LANE), lambda b_, j: (b_, 0, 0)),
                  pl.BlockSpec((None, tk, c), lambda b_, j: (b_, j, 0)),
                  pl.BlockSpec((None, tk, LANE), lambda b_, j: (b_, j, 0)),
                  pl.BlockSpec((None, tn, c), lambda b_, j: (b_, 0, 0)),
                  pl.BlockSpec((None, tn, LANE), lambda b_, j: (b_, 0, 0))],
        out_specs=pl.BlockSpec((None, r, c), lambda b_, j: (b_, 0, 0)),
        out_shape=jax.ShapeDtypeStruct((b, r, c), BF),
        scratch_shapes=[pltpu.VMEM((r, LANE), F32), pltpu.VMEM((r, LANE), F32),
                        pltpu.VMEM((r, c), F32)],
        compiler_params=_params(("parallel", "arbitrary")),
        name="mla_sample_attn",
    )(q_lat, q_rope, cache_lat, cache_rope, c_new, r_new)


def _fox_sample_kernel(q_ref, ck_ref, cv_ref, kn_ref, vn_ref, cc_ref, cn_ref,
                       o_ref, m_sc, l_sc, acc_sc, *, heads):
    j = pl.program_id(1)
    t = q_ref.shape[0]
    head_cols = lambda h: slice(h * HEAD, (h + 1) * HEAD)
    head_rows = lambda h: slice(h * t, (h + 1) * t)

    @pl.when(j == 0)
    def _():
        m_sc[...] = jnp.full(m_sc.shape, NEG, F32)
        l_sc[...] = jnp.zeros(l_sc.shape, F32)
        acc_sc[...] = jnp.zeros(acc_sc.shape, F32)

    def update(keys, vals, neg_cum, causal):
        s = jnp.concatenate(
            [lax.dot_general(q_ref[:, head_cols(h)], keys(h), (((1,), (1,)), ((), ())),
                             preferred_element_type=F32) + neg_cum(h) for h in range(heads)],
            axis=0)
        if causal:
            qp = lax.broadcasted_iota(jnp.int32, s.shape, 0) % t
            kp = lax.broadcasted_iota(jnp.int32, s.shape, 1)
            s = jnp.where(kp <= qp, s, NEG)
        m_prev = m_sc[...]
        m_new = jnp.maximum(m_prev, jnp.max(s, axis=-1, keepdims=True))
        alpha = jnp.exp(m_prev - m_new)
        p = jnp.exp(s - _lanes(m_new, s.shape[1]))
        l_sc[...] = alpha * l_sc[...] + jnp.sum(p, axis=-1, keepdims=True)
        p = p.astype(BF)
        pv = jnp.concatenate(
            [jnp.dot(p[head_rows(h)], vals(h), preferred_element_type=F32) for h in range(heads)],
            axis=0)
        acc_sc[...] = alpha * acc_sc[...] + pv
        m_sc[...] = m_new

    tk = ck_ref.shape[0] // heads
    cache_rows = lambda h: pl.ds(h, tk, stride=heads)
    update(lambda h: ck_ref[cache_rows(h), :].astype(BF),
           lambda h: cv_ref[cache_rows(h), :].astype(BF),
           lambda h: cc_ref[h], False)

    @pl.when(j == pl.num_programs(1) - 1)
    def _():
        update(lambda h: kn_ref[:, head_cols(h)], lambda h: vn_ref[:, head_cols(h)],
               lambda h: cn_ref[h], True)
        o = acc_sc[...] / l_sc[...]
        for h in range(heads):
            o_ref[:, head_cols(h)] = o[head_rows(h)].astype(o_ref.dtype)


def _fox_sample(fq, fk_new, fv_new, cache_k, cache_v, layer, neg_cum_cache, neg_cum_new, heads,
                tk=512):
    b, t, d = fq.shape
    p = cache_k.shape[2]
    tk = _tile(p, tk)
    cache_spec = pl.BlockSpec((None, None, tk * heads, HEAD), lambda b_, j: (layer, b_, j, 0))
    flat = lambda c: c.reshape(c.shape[0], b, p * heads, HEAD)
    return pl.pallas_call(
        functools.partial(_fox_sample_kernel, heads=heads),
        grid=(b, p // tk),
        in_specs=[pl.BlockSpec((None, t, d), lambda b_, j: (b_, 0, 0)),
                  cache_spec, cache_spec,
                  pl.BlockSpec((None, t, d), lambda b_, j: (b_, 0, 0)),
                  pl.BlockSpec((None, t, d), lambda b_, j: (b_, 0, 0)),
                  pl.BlockSpec((None, heads, 1, tk), lambda b_, j: (b_, 0, 0, j)),
                  pl.BlockSpec((None, heads, 1, t), lambda b_, j: (b_, 0, 0, 0))],
        out_specs=pl.BlockSpec((None, t, d), lambda b_, j: (b_, 0, 0)),
        out_shape=jax.ShapeDtypeStruct((b, t, d), BF),
        scratch_shapes=[pltpu.VMEM((heads * t, LANE), F32), pltpu.VMEM((heads * t, LANE), F32),
                        pltpu.VMEM((heads * t, HEAD), F32)],
        compiler_params=_params(("parallel", "arbitrary")),
        name="fox_sample_attn",
    )(fq, flat(cache_k), flat(cache_v), fk_new, fv_new, neg_cum_cache, neg_cum_new)


def _out_proj_kernel(x_ref, oa_ref, ob_ref, ga_ref, gb_ref, w_ref, o_ref):
    mix = (ga_ref[...].astype(F32) * oa_ref[...].astype(F32)
           + gb_ref[...].astype(F32) * ob_ref[...].astype(F32)).astype(BF)
    o_ref[...] = x_ref[...] + jnp.dot(mix, w_ref[...], preferred_element_type=F32)


def _out_proj(x, o_a, o_b, gates, w_o, tm=256):
    m, d = x.shape
    tm = _tile(m, tm)
    row = lambda i: (i, 0)
    return pl.pallas_call(
        _out_proj_kernel,
        grid=(m // tm,),
        in_specs=[pl.BlockSpec((tm, d), row), pl.BlockSpec((tm, d), row),
                  pl.BlockSpec((tm, d), row),
                  pl.BlockSpec((tm, d), lambda i: (i, 0)),
                  pl.BlockSpec((tm, d), lambda i: (i, 1)),
                  pl.BlockSpec((d, d), lambda i: (0, 0))],
        out_specs=pl.BlockSpec((tm, d), row),
        out_shape=jax.ShapeDtypeStruct((m, d), F32),
        compiler_params=_params(("parallel",)),
        name="gated_out_proj",
    )(x, o_a, o_b, gates, gates, w_o)


def _new_weight_tile(be_ref):
    i = pl.program_id(1)
    return (i == 0) | (be_ref[i] != be_ref[jnp.maximum(i - 1, 0)])


def _moe_up_kernel(be_ref, nu_ref, x_ref, wg_ref, wu_ref, bg_ref, bu_ref, o_ref, wg_sc, wu_sc):
    used = pl.program_id(1) < nu_ref[0]

    @pl.when(used & _new_weight_tile(be_ref))
    def _():
        wg_sc[...] = wg_ref[...].astype(BF)
        wu_sc[...] = wu_ref[...].astype(BF)

    @pl.when(used)
    def _():
        x = x_ref[...]
        g = jnp.dot(x, wg_sc[...], preferred_element_type=F32) + bg_ref[...]
        u = jnp.dot(x, wu_sc[...], preferred_element_type=F32) + bu_ref[...]
        g = jnp.minimum(g, SWIGLU_LIMIT)
        u = jnp.clip(u, -SWIGLU_LIMIT, SWIGLU_LIMIT)
        o_ref[...] = (g * _sigmoid(SWIGLU_ALPHA * g) * (u + 1.0)).astype(o_ref.dtype)


def _moe_down_kernel(be_ref, nu_ref, a_ref, w_ref, b_ref, o_ref, w_sc):
    used = pl.program_id(1) < nu_ref[0]

    @pl.when(used & _new_weight_tile(be_ref))
    def _():
        w_sc[...] = w_ref[...].astype(BF)

    @pl.when(used)
    def _():
        o_ref[...] = (jnp.dot(a_ref[...], w_sc[...], preferred_element_type=F32)
                      + b_ref[...]).astype(o_ref.dtype)


def _moe_experts(xs, block_exp, n_used, w_gu, b_gu, w_down, b_down, bm, tf=512, tn=1024):
    p, d = xs.shape
    e, _, ff2 = w_gu.shape
    ff = ff2 // 2
    tf = _tile(ff, tf)
    tn = _tile(d, tn)
    nf = ff // tf
    nb = p // bm
    blk = lambda i, nu: jnp.minimum(i, nu[0] - 1)
    exp = lambda i, be, nu: be[blk(i, nu)]
    act = pl.pallas_call(
        _moe_up_kernel,
        grid_spec=pltpu.PrefetchScalarGridSpec(
            num_scalar_prefetch=2,
            grid=(nf, nb),
            in_specs=[pl.BlockSpec((bm, d), lambda f, i, be, nu: (blk(i, nu), 0)),
                      pl.BlockSpec((None, d, tf), lambda f, i, be, nu: (exp(i, be, nu), 0, f)),
                      pl.BlockSpec((None, d, tf), lambda f, i, be, nu: (exp(i, be, nu), 0, nf + f)),
                      pl.BlockSpec((None, 1, tf), lambda f, i, be, nu: (exp(i, be, nu), 0, f)),
                      pl.BlockSpec((None, 1, tf), lambda f, i, be, nu: (exp(i, be, nu), 0, nf + f))],
            out_specs=pl.BlockSpec((bm, tf), lambda f, i, be, nu: (blk(i, nu), f)),
            scratch_shapes=[pltpu.VMEM((d, tf), BF), pltpu.VMEM((d, tf), BF)],
        ),
        out_shape=jax.ShapeDtypeStruct((p, ff), BF),
        compiler_params=_params(("parallel", "arbitrary")),
        name="moe_gate_up",
    )(block_exp, n_used, xs, w_gu, w_gu, b_gu.reshape(e, 1, ff2), b_gu.reshape(e, 1, ff2))
    return pl.pallas_call(
        _moe_down_kernel,
        grid_spec=pltpu.PrefetchScalarGridSpec(
            num_scalar_prefetch=2,
            grid=(d // tn, nb),
            in_specs=[pl.BlockSpec((bm, ff), lambda n, i, be, nu: (blk(i, nu), 0)),
                      pl.BlockSpec((None, ff, tn), lambda n, i, be, nu: (exp(i, be, nu), 0, n)),
                      pl.BlockSpec((None, 1, tn), lambda n, i, be, nu: (exp(i, be, nu), 0, n))],
            out_specs=pl.BlockSpec((bm, tn), lambda n, i, be, nu: (blk(i, nu), n)),
            scratch_shapes=[pltpu.VMEM((ff, tn), BF)],
        ),
        out_shape=jax.ShapeDtypeStruct((p, d), BF),
        compiler_params=_params(("parallel", "arbitrary")),
        name="moe_down",
    )(block_exp, n_used, act, w_down, b_down.reshape(e, 1, d))


def _rows(x, idx):
    return x.at[idx].get(mode='promise_in_bounds', unique_indices=False)


def _moe(xn, logits, w_gu, b_gu, w_down, b_down, bm):
    n, d = xn.shape
    e = w_gu.shape[0]
    top_v, top_i = lax.top_k(logits, TOP_K)
    gates = jax.nn.softmax(top_v, axis=-1)
    nk = n * TOP_K
    a_exp = top_i.reshape(-1).astype(jnp.int32)
    onehot = (a_exp[:, None] == jnp.arange(e, dtype=jnp.int32)[None, :]).astype(jnp.int32)
    rank = jnp.sum((jnp.cumsum(onehot, axis=0) - onehot) * onehot, axis=1)
    counts = jnp.sum(onehot, axis=0)
    padded = (counts + bm - 1) // bm * bm
    pends = jnp.cumsum(padded)
    pstarts = pends - padded
    dest = pstarts[a_exp] + rank
    nb = -(-(nk + e * (bm - 1)) // bm)
    p = nb * bm
    a_tok = jnp.arange(nk, dtype=jnp.int32) // TOP_K
    buf_tok = jnp.full((p,), n, jnp.int32).at[dest].set(a_tok)
    block_start = jnp.arange(nb, dtype=jnp.int32) * bm
    block_exp = jnp.minimum(jnp.sum((pends[None, :] <= block_start[:, None]).astype(jnp.int32), axis=1),
                            e - 1)
    n_used = (pends[-1:] // bm).astype(jnp.int32)
    xpad = jnp.concatenate([xn, jnp.zeros((1, d), xn.dtype)], axis=0)
    xs = jnp.take(xpad, buf_tok, axis=0, mode='clip')
    out = _moe_experts(xs, block_exp, n_used, w_gu, b_gu, w_down, b_down, bm)
    dest = dest.reshape(n, TOP_K)
    return [_rows(out, dest[:, k]) for k in range(TOP_K)], gates


def _ple_kernel(x_ref, y0_ref, y1_ref, y2_ref, y3_ref, rw_ref, p_ref, wg_ref, bg_ref, wp_ref,
                gf_ref, o_ref, *, final):
    rw = rw_ref[...]
    y = lambda r: r[...].astype(F32)
    x = x_ref[...] + ((rw[:, 0:1] * y(y0_ref) + rw[:, 1:2] * y(y1_ref))
                      + (rw[:, 2:3] * y(y2_ref) + rw[:, 3:4] * y(y3_ref)))
    gate = _sigmoid(jnp.dot(x.astype(BF), wg_ref[...], preferred_element_type=F32) + bg_ref[...])
    x = x + gate * jnp.dot(p_ref[...].astype(BF), wp_ref[...], preferred_element_type=F32)
    if final:
        x = _rms(x, gf_ref[...])
    o_ref[...] = x


def _ple(x, p_emb, x_row0, m, ys, route_w, y_row0, w_plg, b_plg, w_pl, g_final, final, tm=256):
    d = x.shape[1]
    pd = p_emb.shape[1]
    tm = _tile(m, tm)
    xr, yr = x_row0 // tm, y_row0 // tm
    xrow = lambda i: (i + xr, 0)
    yrow = lambda i: (i + yr, 0)
    fixed = lambda i: (0, 0)
    return pl.pallas_call(
        functools.partial(_ple_kernel, final=final),
        grid=(m // tm,),
        in_specs=[pl.BlockSpec((tm, d), xrow)] + [pl.BlockSpec((tm, d), yrow)] * TOP_K
                 + [pl.BlockSpec((tm, TOP_K), yrow), pl.BlockSpec((tm, pd), xrow),
                    pl.BlockSpec((d, d), fixed), pl.BlockSpec((1, d), fixed),
                    pl.BlockSpec((pd, d), fixed), pl.BlockSpec((1, d), fixed)],
        out_specs=pl.BlockSpec((tm, d), lambda i: (i, 0)),
        out_shape=jax.ShapeDtypeStruct((m, d), F32),
        compiler_params=_params(("parallel",)),
        name="ple_gate_final",
    )(x, *ys, route_w, p_emb, w_plg, b_plg.reshape(1, d), w_pl, g_final.reshape(1, d))


def _rot_cols(w):
    half = w.shape[-1] // 2
    return jnp.concatenate([-w[..., half:], w[..., :half]], axis=-1)


def _rope_table(pos):
    half = ROPE // 2
    freqs = ROPE_THETA ** (-jnp.arange(half, dtype=F32) / half)
    ang = pos.astype(F32)[:, None] * freqs[None, :]
    cos, sin = jnp.cos(ang), jnp.sin(ang)
    return jnp.concatenate([cos, cos, sin, sin], axis=-1)


def _layer_weights(w_in, b_forget, b_gate, w_q_up, w_uk, w_uv, heads, ql, kvl):
    d = w_in.shape[0]
    o = 0
    parts = []
    for s in (ql, kvl, ROPE, d, d, d, heads, 2 * d):
        parts.append(w_in[:, o:o + s])
        o += s
    w_cq, w_ckv, w_kr, w_fq, w_fk, w_fv, w_fg, w_gate = parts
    w_lat = jnp.concatenate(
        [w_cq, w_ckv, w_kr, _rot_cols(w_kr), jnp.pad(w_fg, ((0, 0), (0, LANE - heads)))], axis=1)
    wq_nope = w_q_up[:, :, :HEAD]
    wq_rope = w_q_up[:, :, HEAD:]
    wq = jnp.concatenate([wq_nope, wq_rope, _rot_cols(wq_rope)], axis=-1).reshape(ql, heads * QK_PAD)
    return dict(
        w_lat=w_lat.astype(BF), w_fq=w_fq.astype(BF), w_fk=w_fk.astype(BF), w_fv=w_fv.astype(BF),
        w_gate=w_gate.astype(BF), wq=wq.astype(BF),
        w_uk=w_uk.reshape(kvl, heads * HEAD).astype(BF),
        w_uv=w_uv.reshape(kvl, heads * HEAD).astype(BF),
        w_uk_t=jnp.transpose(w_uk, (1, 2, 0)).astype(BF),
        w_uv_h=jnp.transpose(w_uv, (1, 0, 2)).astype(BF),
        b_forget=jnp.pad(b_forget, (0, LANE - heads)).reshape(1, LANE),
        b_gate=b_gate.reshape(1, 2 * d))


def _cumsum_last(x):
    *lead, s = x.shape
    sp = -(-s // LANE) * LANE
    xp = jnp.pad(x, [(0, 0)] * len(lead) + [(0, sp - s)]).reshape(-1, sp // LANE, LANE)
    tri = (jnp.arange(LANE)[:, None] <= jnp.arange(LANE)[None, :]).astype(F32)
    within = jnp.einsum('rcl,lm->rcm', xp, tri, precision=lax.Precision.HIGHEST)
    totals = within[:, :, -1]
    offsets = jnp.cumsum(totals, axis=1) - totals
    return (within + offsets[:, :, None]).reshape(*lead, sp)[..., :s]


def _trunc_bf16(x):
    bits = lax.bitcast_convert_type(x, jnp.uint32) & jnp.uint32(0xFFFF0000)
    return lax.bitcast_convert_type(bits, F32)


def _forget_bias_columns(logf, b, t, heads):
    cum = jnp.cumsum(logf.reshape(b, t, heads), axis=1)
    nck = -LOG2E * cum
    hi = _trunc_bf16(nck)
    mid = _trunc_bf16(nck - hi)
    lo = nck - hi - mid
    ext = jnp.stack([hi, mid, lo], axis=-1).astype(BF)
    ext = jnp.pad(ext, ((0, 0), (0, 0), (0, 0), (0, HEAD - 3)))
    return ext.reshape(b * t, heads * HEAD)


def _mixer_inputs(x, cs, lw, g_mix, g_q_lat, g_kv_lat, heads, ql, kvl, tm, prompt_bt=None):
    n, d = x.shape
    prompt = prompt_bt is not None
    h = _rmsnorm(x, g_mix, BF)
    tm = _tile(n, tm)
    row = lambda i, j: (i, 0)
    fixed = lambda i, j: (0, 0)
    wl = ql + kvl + 2 * LANE
    cqn, clat, clatb, kr, krb, logf = _mm(
        h, lw['w_lat'], functools.partial(_latent_epilogue, ql=ql, kvl=kvl),
        [jax.ShapeDtypeStruct((n, ql), BF), jax.ShapeDtypeStruct((n, kvl), F32),
         jax.ShapeDtypeStruct((n, kvl), BF), jax.ShapeDtypeStruct((n, LANE), F32),
         jax.ShapeDtypeStruct((n, LANE), BF), jax.ShapeDtypeStruct((n, LANE), F32)],
        [ql, kvl, kvl, LANE, LANE, LANE], tm=tm, tn=wl,
        extra=(g_q_lat.reshape(1, ql), g_kv_lat.reshape(1, kvl), lw['b_forget'], cs),
        extra_specs=(pl.BlockSpec((1, ql), fixed), pl.BlockSpec((1, kvl), fixed),
                     pl.BlockSpec((1, LANE), fixed), pl.BlockSpec((tm, LANE), row)),
        name="latent_proj")
    tn = _tile(d, 512)
    hpt = tn // HEAD
    if prompt:
        (fq,) = _mm(h, lw['w_fq'],
                    functools.partial(_fox_q_aug_epilogue, heads_per_tile=hpt,
                                      scale=FOX_SCALE * LOG2E),
                    [jax.ShapeDtypeStruct((n, heads * QK_PAD), BF)], [hpt * QK_PAD],
                    tm=tm, tn=tn, name="fox_q_proj")
        ext = _forget_bias_columns(logf[:, :heads], *prompt_bt, heads)
        fk, fkb = _mm(h, lw['w_fk'], functools.partial(_fox_k_aug_epilogue, heads_per_tile=hpt),
                      [jax.ShapeDtypeStruct((n, d), F32),
                       jax.ShapeDtypeStruct((n, heads * QK_PAD), BF)],
                      [tn, hpt * QK_PAD], tm=tm, tn=tn, extra=(ext,),
                      extra_specs=(pl.BlockSpec((tm, tn), lambda i, j: (i, j)),),
                      name="fox_k_proj")
    else:
        (fq,) = _mm(h, lw['w_fq'], functools.partial(_scaled_bf16_epilogue, scale=FOX_SCALE),
                    [jax.ShapeDtypeStruct((n, d), BF)], [tn], tm=tm, tn=tn, name="fox_q_proj")
        fk, fkb = _mm(h, lw['w_fk'], _dual_epilogue,
                      [jax.ShapeDtypeStruct((n, d), F32), jax.ShapeDtypeStruct((n, d), BF)],
                      [tn, tn], tm=tm, tn=tn, name="fox_k_proj")
    fv, fvb = _mm(h, lw['w_fv'], _dual_epilogue,
                  [jax.ShapeDtypeStruct((n, d), F32), jax.ShapeDtypeStruct((n, d), BF)],
                  [tn, tn], tm=tm, tn=tn, name="fox_v_proj")
    (gates,) = _mm(h, lw['w_gate'], _gate_epilogue, [jax.ShapeDtypeStruct((n, 2 * d), BF)], [tn],
                   tm=tm, tn=tn, extra=(lw['b_gate'],),
                   extra_specs=(pl.BlockSpec((1, tn), lambda i, j: (0, j)),), name="merge_gates")
    qh = 2 if heads % 2 == 0 else 1
    (q,) = _mm(cqn, lw['wq'],
               functools.partial(_q_epilogue, heads_per_tile=qh,
                                 scale=MLA_SCALE * LOG2E if prompt else MLA_SCALE),
               [jax.ShapeDtypeStruct((n, heads * QK_PAD), BF)], [qh * QK_PAD],
               tm=tm, tn=qh * QK_PAD, extra=(cs,),
               extra_specs=(pl.BlockSpec((tm, LANE), row),), name="mla_q_up")
    return dict(q=q, clat=clat, clatb=clatb, kr=kr, krb=krb, logf=logf, fq=fq, fk=fk, fkb=fkb,
                fv=fv, fvb=fvb, gates=gates)


def _prompt_attention(mi, lw, b, t, heads, kvl, tm):
    n = b * t
    d = heads * HEAD
    hpt = 2 if heads % 2 == 0 else 1
    tm = _tile(n, tm)
    (k,) = _mm(mi['clatb'], lw['w_uk'], functools.partial(_k_epilogue, heads_per_tile=hpt),
               [jax.ShapeDtypeStruct((n, heads * QK_PAD), BF)], [hpt * QK_PAD],
               tm=tm, tn=hpt * HEAD, extra=(mi['krb'],),
               extra_specs=(pl.BlockSpec((tm, LANE), lambda i, j: (i, 0)),), name="mla_k_up")
    (v,) = _mm(mi['clatb'], lw['w_uv'], _bf16_epilogue, [jax.ShapeDtypeStruct((n, d), BF)],
               [_tile(d, 512)], tm=tm, tn=_tile(d, 512), name="mla_v_up")
    o_a = _flash_prompt(mi['q'].reshape(b, t, -1), k.reshape(b, t, -1), v.reshape(b, t, -1),
                        heads, causal=False)
    o_b = _flash_prompt(mi['fq'].reshape(b, t, -1), mi['fkb'].reshape(b, t, -1),
                        mi['fvb'].reshape(b, t, -1), heads, causal=True)
    return o_a.reshape(n, d), o_b.reshape(n, d)


def _sample_attention(mi, lw, caches, layer, b, t, heads, kvl, tm):
    cache_lat, cache_rope, cache_k, cache_v, cache_logf = caches
    n = b * t
    d = heads * HEAD
    p = cache_lat.shape[2]
    tm = _tile(n, tm)
    (q_lat,) = _mm(mi['q'], lw['w_uk_t'], _bf16_epilogue,
                   [jax.ShapeDtypeStruct((n, heads * kvl), BF)], [kvl], tm=tm, tn=kvl,
                   a_spec=pl.BlockSpec((tm, HEAD), lambda i, j: (i, 2 * j)),
                   w_spec=pl.BlockSpec((None, HEAD, kvl), lambda i, j: (j, 0, 0)),
                   n_col_tiles=heads, name="mla_q_absorb")
    q_rope = mi['q'].reshape(n, heads, QK_PAD)[:, :, HEAD:]
    cache_rope_p = jnp.pad(cache_rope[layer], ((0, 0), (0, 0), (0, LANE - ROPE)))
    o_lat = _mla_sample(q_lat.reshape(b, t * heads, kvl), q_rope.reshape(b, t * heads, LANE),
                        cache_lat[layer], cache_rope_p, mi['clatb'].reshape(b, t, kvl),
                        mi['krb'].reshape(b, t, LANE))
    (o_a,) = _mm(o_lat.reshape(n, heads * kvl), lw['w_uv_h'], _bf16_epilogue,
                 [jax.ShapeDtypeStruct((n, d), BF)], [HEAD], tm=tm, tn=HEAD,
                 a_spec=pl.BlockSpec((tm, kvl), lambda i, j: (i, j)),
                 w_spec=pl.BlockSpec((None, kvl, HEAD), lambda i, j: (j, 0, 0)),
                 n_col_tiles=heads, name="mla_v_absorb")
    logf = mi['logf'][:, :heads].reshape(b, t, heads)
    logf_all = jnp.concatenate([cache_logf[layer].astype(F32), logf], axis=1)
    neg_cum = -_cumsum_last(logf_all.transpose(0, 2, 1))
    o_b = _fox_sample(mi['fq'].reshape(b, t, d), mi['fkb'].reshape(b, t, d),
                      mi['fvb'].reshape(b, t, d), cache_k, cache_v, layer,
                      neg_cum[:, :, None, :p], neg_cum[:, :, None, p:], heads)
    return o_a, o_b.reshape(n, d)


def kernel(x_prompt, x_sample, p_prompt, p_sample, cache_mla_latent, cache_mla_rope, cache_fox_k,
           cache_fox_v, cache_fox_logf, g_mix, w_in, b_forget, b_gate, g_q_lat, w_q_up, g_kv_lat,
           w_uk, w_uv, w_o, g_ffn, w_router, b_router, w_gu, b_gu, w_down, b_down, w_pl, w_plg,
           b_plg, g_final):
    bp, tp, d = x_prompt.shape
    bs, ts, _ = x_sample.shape
    depth = g_mix.shape[0]
    heads = d // HEAD
    ql = w_q_up.shape[1]
    kvl = w_uk.shape[1]
    past = cache_mla_latent.shape[2]
    n_p, n_s = bp * tp, bs * ts
    tm = 1024
    bm = 512

    cs_p = jnp.tile(_rope_table(jnp.arange(tp)), (bp, 1))
    cs_s = jnp.tile(_rope_table(past + jnp.arange(ts)), (bs, 1))
    x_p = x_prompt.reshape(n_p, d)
    x_s = x_sample.reshape(n_s, d)
    caches = (cache_mla_latent, cache_mla_rope, cache_fox_k, cache_fox_v, cache_fox_logf)
    st_p, st_s = [], []
    for i in range(depth):
        last = i == depth - 1
        lw = _layer_weights(w_in[i], b_forget[i], b_gate[i], w_q_up[i], w_uk[i], w_uv[i],
                            heads, ql, kvl)
        w_o_b = w_o[i].astype(BF)
        mi_p = _mixer_inputs(x_p, cs_p, lw, g_mix[i], g_q_lat[i], g_kv_lat[i], heads, ql, kvl, tm,
                             prompt_bt=(bp, tp))
        mi_s = _mixer_inputs(x_s, cs_s, lw, g_mix[i], g_q_lat[i], g_kv_lat[i], heads, ql, kvl, tm)
        oa_p, ob_p = _prompt_attention(mi_p, lw, bp, tp, heads, kvl, tm)
        oa_s, ob_s = _sample_attention(mi_s, lw, caches, i, bs, ts, heads, kvl, tm)
        x_p = _out_proj(x_p, oa_p, ob_p, mi_p['gates'], w_o_b)
        x_s = _out_proj(x_s, oa_s, ob_s, mi_s['gates'], w_o_b)
        xn_p, lg_p = _ffn_norm(x_p, g_ffn[i], w_router[i], b_router[i])
        xn_s, lg_s = _ffn_norm(x_s, g_ffn[i], w_router[i], b_router[i])
        w_plg_b = w_plg[i].astype(BF)
        w_pl_b = w_pl[i].astype(BF)
        pp = p_prompt[i].reshape(n_p, -1)
        ps = p_sample[i].reshape(n_s, -1)
        experts = lambda xn, lg: _moe(xn, lg, w_gu[i], b_gu[i], w_down[i], b_down[i], bm)
        ple = lambda x, pe, x0, m, ys, rw, y0: _ple(x, pe, x0, m, ys, rw, y0, w_plg_b, b_plg[i],
                                                    w_pl_b, g_final, last, tm=PLE_ROWS)
        ys, rw = experts(jnp.concatenate([xn_p, xn_s], axis=0),
                         jnp.concatenate([lg_p, lg_s], axis=0))
        x_p = ple(x_p, pp, 0, n_p, ys, rw, 0)
        x_s = ple(x_s, ps, 0, n_s, ys, rw, n_p)
        for st, mi, b, t in ((st_p, mi_p, bp, tp), (st_s, mi_s, bs, ts)):
            st.append((mi['clat'].reshape(b, t, kvl),
                       mi['kr'][:, :ROPE].reshape(b, t, ROPE),
                       mi['fk'].reshape(b, t, heads, HEAD),
                       mi['fv'].reshape(b, t, heads, HEAD),
                       mi['logf'][:, :heads].reshape(b, t, heads)))
    outs = [x_p.reshape(bp, tp, d), x_s.reshape(bs, ts, d)]
    for st in (st_p, st_s):
        for k in range(5):
            outs.append(jnp.stack([s[k] for s in st]))
    return tuple(outs)
```
